```python
import jax, jax.numpy as jnp
from jax import lax
import numpy as np

D_MODEL = 2048
BATCH = 8
SEQ = 2048
DEPTH = 2

N_HEADS_TOTAL = 16
HEAD_DIM = D_MODEL // N_HEADS_TOTAL
DIL_PATTERNS = ((128, 1), (512, 4), (2048, 16))
N_DIL_GROUPS = len(DIL_PATTERNS)
HEADS_PER_GROUP = 4
A_HEADS = N_DIL_GROUPS * HEADS_PER_GROUP
A_OUT = HEADS_PER_GROUP * HEAD_DIM
SB_HEADS = 4
B_OUT = SB_HEADS * HEAD_DIM
A_W = A_HEADS * HEAD_DIM
IN_SPLITS = (A_W, A_W, A_W, B_OUT, B_OUT, B_OUT, D_MODEL, D_MODEL)
IN_WIDTH = sum(IN_SPLITS)
D_FF = -(-8 * D_MODEL // (3 * 256)) * 256
BLOCK = 128
ROPE_THETA = 10000.0
EPS = 1e-6

kernel_name = "hybrid_dilated_stickbreaking_adaln_block"


def rms_norm(x, g):
    xf = x.astype(jnp.float32)
    y = xf * lax.rsqrt(jnp.mean(xf * xf, axis=-1, keepdims=True) + EPS)
    return y * g.astype(jnp.float32)


def rope_tables(seq):
    inv = jnp.power(ROPE_THETA, -jnp.arange(0, HEAD_DIM, 2, dtype=jnp.float32) / HEAD_DIM)
    ang = jnp.arange(seq, dtype=jnp.float32)[:, None] * inv[None, :]
    return jnp.cos(ang), jnp.sin(ang)


def apply_rope(x, cos, sin):
    half = HEAD_DIM // 2
    x1, x2 = x[..., :half], x[..., half:]
    cs, sn = cos[None, :, None, :], sin[None, :, None, :]
    return jnp.concatenate([x1 * cs - x2 * sn, x2 * cs + x1 * sn], axis=-1)


def dilated_window_attention(q, k, v, window, dilation):
    B, S, H, hd = q.shape
    L = S // dilation
    w_sub = window // dilation
    Lp = -(-L // BLOCK) * BLOCK
    nb = Lp // BLOCK

    def to_sub(t):
        t = t.reshape(B, L, dilation, H, hd).transpose(0, 2, 3, 1, 4)
        t = jnp.pad(t, ((0, 0), (0, 0), (0, 0), (0, Lp - L), (0, 0)))
        return t.reshape(B, dilation, H, nb, BLOCK, hd)

    qb, kb, vb = to_sub(q), to_sub(k), to_sub(v)

    def with_prev(t):
        prev = jnp.pad(t, ((0, 0), (0, 0), (0, 0), (1, 0), (0, 0), (0, 0)))[:, :, :, :nb]
        return jnp.concatenate([prev, t], axis=4)

    kk, vv = with_prev(kb), with_prev(vb)
    s = jnp.einsum('brhnqd,brhnkd->brhnqk', qb, kk) * (hd ** -0.5)
    qi = jnp.arange(BLOCK)[:, None]
    kj = jnp.arange(2 * BLOCK)[None, :]
    dist = qi + BLOCK - kj
    band = (dist >= 0) & (dist <= w_sub)
    valid = (jnp.arange(nb)[:, None, None] * BLOCK + kj[None] - BLOCK) >= 0
    mask = band[None] & valid
    s = jnp.where(mask, s, -jnp.inf)
    m = jnp.max(s, axis=-1, keepdims=True)
    p = jnp.exp(s - m)
    den = jnp.sum(p, axis=-1)
    o = jnp.einsum('brhnqk,brhnkd->brhnqd', p, vv) / den[..., None]
    lse = m[..., 0] + jnp.log(den)
    o = o.reshape(B, dilation, H, Lp, hd)[:, :, :, :L].transpose(0, 3, 1, 2, 4).reshape(B, S, H, hd)
    lse = lse.reshape(B, dilation, H, Lp)[:, :, :, :L].transpose(0, 3, 1, 2).reshape(B, S, H)
    return o, lse


def mixer_a(qa, ka, va, qn_g, kn_g, cos, sin):
    B, S = qa.shape[0], qa.shape[1]
    q = apply_rope(rms_norm(qa, qn_g), cos, sin)
    k = apply_rope(rms_norm(ka, kn_g), cos, sin)
    v = va.astype(jnp.float32)
    outs, lses = [], []
    for g, (window, dilation) in enumerate(DIL_PATTERNS):
        lo, hi = g * HEADS_PER_GROUP, (g + 1) * HEADS_PER_GROUP
        o, lse = dilated_window_attention(q[:, :, lo:hi], k[:, :, lo:hi], v[:, :, lo:hi], window, dilation)
        outs.append(o)
        lses.append(lse)
    o = jnp.stack(outs, 0)
    w = jax.nn.softmax(jnp.stack(lses, 0), axis=0)
    return jnp.sum(w[..., None] * o, axis=0).reshape(B, S, A_OUT)


def stick_breaking_attention(q, k, v):
    B, S, H, hd = q.shape
    nb = S // BLOCK
    qh = q.astype(jnp.float32).transpose(0, 2, 1, 3)
    kh = k.astype(jnp.float32).transpose(0, 2, 1, 3)
    vh = v.astype(jnp.float32).transpose(0, 2, 1, 3)
    qblocks = qh.reshape(B, H, nb, BLOCK, hd).transpose(2, 0, 1, 3, 4)
    key_pos = jnp.arange(S)
    scale = hd ** -0.5

    def one_block(args):
        qblk, i = args
        z = jnp.einsum('bhqd,bhkd->bhqk', qblk, kh) * scale
        qpos = i * BLOCK + jnp.arange(BLOCK)
        causal = key_pos[None, :] < qpos[:, None]
        log_beta = jax.nn.log_sigmoid(z)
        log_1mb = jnp.where(causal, jax.nn.log_sigmoid(-z), 0.0)
        shifted = jnp.concatenate([log_1mb[..., 1:], jnp.zeros_like(log_1mb[..., :1])], axis=-1)
        after = lax.cumsum(shifted, axis=3, reverse=True)
        a = jnp.where(causal, jnp.exp(log_beta + after), 0.0)
        return jnp.einsum('bhqk,bhkd->bhqd', a, vh)

    o = lax.map(one_block, (qblocks, jnp.arange(nb)))
    return o.transpose(1, 0, 3, 2, 4).reshape(B, S, H * hd)


def setup_inputs(seed: int = 0) -> dict:
    key = jax.random.key(seed)
    ks = jax.random.split(key, 16)
    f32 = jnp.float32

    def w(k, shape, fan_in, mult=1.0):
        return jax.random.normal(k, shape, f32) * (mult * fan_in ** -0.5)

    return {
        "x": jax.random.normal(ks[0], (BATCH, SEQ, D_MODEL), f32),
        "c": jax.random.normal(ks[1], (BATCH, D_MODEL), f32),
        "w_ada": w(ks[2], (DEPTH, D_MODEL, 6 * D_MODEL), D_MODEL, 0.5),
        "b_ada": 0.01 * jax.random.normal(ks[3], (DEPTH, 6 * D_MODEL), f32),
        "norm1_g": 1.0 + 0.02 * jax.random.normal(ks[4], (DEPTH, D_MODEL), f32),
        "norm2_g": 1.0 + 0.02 * jax.random.normal(ks[5], (DEPTH, D_MODEL), f32),
        "w_in": w(ks[6], (DEPTH, D_MODEL, IN_WIDTH), D_MODEL),
        "qn_g": 1.0 + 0.02 * jax.random.normal(ks[7], (DEPTH, HEAD_DIM), f32),
        "kn_g": 1.0 + 0.02 * jax.random.normal(ks[8], (DEPTH, HEAD_DIM), f32),
        "w_branch_a": w(ks[9], (DEPTH, A_OUT, D_MODEL), A_OUT),
        "w_branch_b": w(ks[10], (DEPTH, B_OUT, D_MODEL), B_OUT),
        "w_out": w(ks[11], (DEPTH, D_MODEL, D_MODEL), D_MODEL),
        "w_gate_up": w(ks[12], (DEPTH, D_MODEL, 2 * D_FF), D_MODEL),
        "w_down": w(ks[13], (DEPTH, D_FF, D_MODEL), D_FF),
    }


def reference(x, c, w_ada, b_ada, norm1_g, norm2_g, w_in, qn_g, kn_g,
              w_branch_a, w_branch_b, w_out, w_gate_up, w_down):
    B, S, D = x.shape
    cos, sin = rope_tables(S)
    offs = [0]
    for n in IN_SPLITS:
        offs.append(offs[-1] + n)
    c_act = jax.nn.silu(c)
    h = x
    for l in range(DEPTH):
        mod = (c_act @ w_ada[l] + b_ada[l])[:, None, :]
        shift1 = mod[..., 0 * D:1 * D]
        scale1 = mod[..., 1 * D:2 * D]
        gate1 = mod[..., 2 * D:3 * D]
        shift2 = mod[..., 3 * D:4 * D]
        scale2 = mod[..., 4 * D:5 * D]
        gate2 = mod[..., 5 * D:6 * D]

        u = (rms_norm(h, norm1_g[l]) * (1.0 + scale1) + shift1).astype(h.dtype)
        proj = u @ w_in[l]
        parts = [proj[..., offs[i]:offs[i + 1]] for i in range(len(IN_SPLITS))]
        qa = parts[0].reshape(B, S, A_HEADS, HEAD_DIM)
        ka = parts[1].reshape(B, S, A_HEADS, HEAD_DIM)
        va = parts[2].reshape(B, S, A_HEADS, HEAD_DIM)
        qb = parts[3].reshape(B, S, SB_HEADS, HEAD_DIM)
        kb = parts[4].reshape(B, S, SB_HEADS, HEAD_DIM)
        vb = parts[5].reshape(B, S, SB_HEADS, HEAD_DIM)
        ga, gb = parts[6], parts[7]
        o_a = mixer_a(qa, ka, va, qn_g[l], kn_g[l], cos, sin).astype(h.dtype)
        o_b = stick_breaking_attention(qb, kb, vb).astype(h.dtype)
        y_a = o_a @ w_branch_a[l]
        y_b = o_b @ w_branch_b[l]
        merged = jax.nn.sigmoid(ga) * y_a + jax.nn.sigmoid(gb) * y_b
        h = h + gate1 * (merged @ w_out[l])

        u2 = (rms_norm(h, norm2_g[l]) * (1.0 + scale2) + shift2).astype(h.dtype)
        gu = u2 @ w_gate_up[l]
        h = h + gate2 * ((jax.nn.silu(gu[..., :D_FF]) * gu[..., D_FF:]) @ w_down[l])
    return h
```

```python
import functools

import jax
import jax.numpy as jnp
from jax import lax
from jax.experimental import pallas as pl
from jax.experimental.pallas import tpu as pltpu

F32 = jnp.float32
BF16 = jnp.bfloat16

HEAD_DIM = 128
HEADS_PER_GROUP = 4
GROUP_W = HEADS_PER_GROUP * HEAD_DIM
DIL_PATTERNS = ((128, 1), (512, 4), (2048, 16))
WIN_BLOCK = 128
ROPE_THETA = 10000.0
EPS = 1e-6
QK_SCALE = HEAD_DIM ** -0.5

VMEM_LIMIT_BYTES = 56 * 1024 * 1024


def _params(*semantics):
    return pltpu.CompilerParams(dimension_semantics=semantics,
                                vmem_limit_bytes=VMEM_LIMIT_BYTES)


def _dot(a, b):
    return jnp.dot(a, b, preferred_element_type=F32)


def _dot_nt(a, b):
    return lax.dot_general(a, b, (((1,), (1,)), ((), ())), preferred_element_type=F32)


def _mod_kernel(c_ref, w_ref, b_ref, o_ref):
    c = c_ref[...]
    c_act = (c * jax.nn.sigmoid(c)).astype(BF16)
    o_ref[...] = _dot(c_act, w_ref[...].astype(BF16)) + b_ref[...]


def _modulation(c, w_ada, b_ada, tn=512):
    depth, d_model, n = w_ada.shape
    bsz = c.shape[0]
    return pl.pallas_call(
        _mod_kernel,
        grid=(depth, n // tn),
        in_specs=[
            pl.BlockSpec((bsz, d_model), lambda l, j: (0, 0)),
            pl.BlockSpec((None, d_model, tn), lambda l, j: (l, 0, j)),
            pl.BlockSpec((None, 1, tn), lambda l, j: (l, 0, j)),
        ],
        out_specs=pl.BlockSpec((None, bsz, tn), lambda l, j: (l, 0, j)),
        out_shape=jax.ShapeDtypeStruct((depth, bsz, n), F32),
        compiler_params=_params("parallel", "parallel"),
        name="modulation",
    )(c, w_ada, b_ada.reshape(depth, 1, n))


def _modulated_norm(h, g, scale, shift):
    ms = jnp.mean(h * h, axis=-1, keepdims=True)
    return (h * lax.rsqrt(ms + EPS)) * g * (1.0 + scale) + shift


def _norm_kernel(h_ref, g_ref, scale_ref, shift_ref, o_ref):
    o_ref[...] = _modulated_norm(h_ref[...], g_ref[...], scale_ref[...],
                                 shift_ref[...]).astype(BF16)


def _norm_modulate(h, g, mod3, shift_idx, scale_idx, seq, tm=512):
    tokens, d_model = h.shape
    blocks_per_seq = seq // tm
    return pl.pallas_call(
        _norm_kernel,
        grid=(tokens // tm,),
        in_specs=[
            pl.BlockSpec((tm, d_model), lambda i: (i, 0)),
            pl.BlockSpec((1, d_model), lambda i: (0, 0)),
            pl.BlockSpec((None, 1, d_model), lambda i: (i // blocks_per_seq, 0, scale_idx)),
            pl.BlockSpec((None, 1, d_model), lambda i: (i // blocks_per_seq, 0, shift_idx)),
        ],
        out_specs=pl.BlockSpec((tm, d_model), lambda i: (i, 0)),
        out_shape=jax.ShapeDtypeStruct((tokens, d_model), BF16),
        compiler_params=_params("parallel"),
        name="norm_modulate",
    )(h, g, mod3, mod3)


def _qk_epilogue(acc, gain, cos_ref, sin_ref, o_ref):
    tm = acc.shape[0]
    rows = cos_ref.shape[0]
    cos = cos_ref[...]
    sin = sin_ref[...]
    for seg in range(tm // rows):
        r0 = seg * rows
        for hh in range(HEADS_PER_GROUP):
            c0 = hh * HEAD_DIM
            x = acc[r0:r0 + rows, c0:c0 + HEAD_DIM]
            ms = jnp.mean(x * x, axis=-1, keepdims=True)
            y = (x * lax.rsqrt(ms + EPS)) * gain
            out = y * cos + pltpu.roll(y, HEAD_DIM // 2, 1) * sin
            o_ref[r0:r0 + rows, c0:c0 + HEAD_DIM] = out.astype(BF16)


def _proj_kernel(u_ref, w_ref, cos_ref, sin_ref, qg_ref, kg_ref, o_ref, *, n_gate):
    j = pl.program_id(2)
    acc = _dot(u_ref[...], w_ref[...])

    if n_gate:
        @pl.when(j < n_gate)
        def _():
            o_ref[...] = jax.nn.sigmoid(acc).astype(BF16)

    @pl.when(j == n_gate)
    def _():
        _qk_epilogue(acc, qg_ref[...] * QK_SCALE, cos_ref, sin_ref, o_ref)

    @pl.when(j == n_gate + 1)
    def _():
        _qk_epilogue(acc, kg_ref[...], cos_ref, sin_ref, o_ref)

    if n_gate:
        @pl.when(j == n_gate + 3)
        def _():
            o_ref[...] = (acc * QK_SCALE).astype(BF16)

        @pl.when((j == n_gate + 2) | (j > n_gate + 3))
        def _():
            o_ref[...] = acc.astype(BF16)
    else:
        @pl.when(j >= n_gate + 2)
        def _():
            o_ref[...] = acc.astype(BF16)


def _gate_tiles(group, d_model):
    return 2 * d_model // GROUP_W if group == 0 else 0


def _input_projection(u, w, cos_full, sin_full, qn_g, kn_g, *, bsz, seq, group, tm=1024):
    tokens, d_model = u.shape
    dil = DIL_PATTERNS[group][1]
    sub_len = seq // dil
    rows_total = tokens // dil
    tm = min(tm, rows_total)
    n_groups = len(DIL_PATTERNS)
    n_gate = _gate_tiles(group, d_model)
    if group == 0:
        n_tiles = n_gate + 6
        a_tiles = 3 * n_groups

        def col_of(j):
            jj = j - n_gate
            qkv0_or_b = jnp.where(jj < 3, n_groups * jj, jj - 3 + a_tiles)
            return jnp.where(j < n_gate, j + a_tiles + 3, qkv0_or_b)
    else:
        n_tiles = 3
        col_of = lambda j: n_groups * j + group
    rope_rows = min(tm, sub_len)
    rope_blocks = sub_len // rope_rows
    kern = functools.partial(_proj_kernel, n_gate=n_gate)
    return pl.pallas_call(
        kern,
        grid=(dil, rows_total // tm, n_tiles),
        in_specs=[
            pl.BlockSpec((tm, d_model), lambda r, i, j: (i, r)),
            pl.BlockSpec((d_model, GROUP_W), lambda r, i, j: (0, col_of(j))),
            pl.BlockSpec((rope_rows, HEAD_DIM), lambda r, i, j: (i % rope_blocks, r)),
            pl.BlockSpec((rope_rows, HEAD_DIM), lambda r, i, j: (i % rope_blocks, r)),
            pl.BlockSpec((1, HEAD_DIM), lambda r, i, j: (0, 0)),
            pl.BlockSpec((1, HEAD_DIM), lambda r, i, j: (0, 0)),
        ],
        out_specs=pl.BlockSpec((None, tm, GROUP_W), lambda r, i, j: (r, i, j)),
        out_shape=jax.ShapeDtypeStruct((dil, rows_total, n_tiles * GROUP_W), BF16),
        compiler_params=_params("parallel", "parallel", "arbitrary"),
        name=f"input_projection_g{group}",
    )(u.reshape(rows_total, dil * d_model), w,
      cos_full.reshape(sub_len, dil * HEAD_DIM), sin_full.reshape(sub_len, dil * HEAD_DIM),
      qn_g, kn_g)


def _window_kernel(q_ref, k_ref, v_ref, kp_ref, vp_ref, o_ref, lse_ref, *, sub_len):
    rb = q_ref.shape[0]
    blocks_per_sub = max(sub_len // rb, 1)
    starts_sub = (pl.program_id(0) % blocks_per_sub) == 0
    qi_ = lax.broadcasted_iota(jnp.int32, (WIN_BLOCK, WIN_BLOCK), 0)
    kj_ = lax.broadcasted_iota(jnp.int32, (WIN_BLOCK, WIN_BLOCK), 1)
    cur_mask = kj_ <= qi_
    prev_mask = kj_ >= qi_
    halo_mask = kj_ >= qi_ + jnp.where(starts_sub, WIN_BLOCK, 0)
    lse_ref[...] = jnp.zeros(lse_ref.shape, F32)
    for qb in range(rb // WIN_BLOCK):
        r0 = qb * WIN_BLOCK
        first_in_sub = (r0 % sub_len) == 0
        for hh in range(HEADS_PER_GROUP):
            cs = slice(hh * HEAD_DIM, (hh + 1) * HEAD_DIM)
            q = q_ref[r0:r0 + WIN_BLOCK, cs]
            k_cur = k_ref[r0:r0 + WIN_BLOCK, cs]
            v_cur = v_ref[r0:r0 + WIN_BLOCK, cs]
            s_cur = jnp.where(cur_mask, _dot_nt(q, k_cur), -jnp.inf)
            use_prev = not (first_in_sub and sub_len <= rb)
            if use_prev:
                if qb == 0:
                    k_prev = kp_ref[:, cs]
                    v_prev = vp_ref[:, cs]
                    pmask = halo_mask
                else:
                    k_prev = k_ref[r0 - WIN_BLOCK:r0, cs]
                    v_prev = v_ref[r0 - WIN_BLOCK:r0, cs]
                    pmask = prev_mask
                s_prev = jnp.where(pmask, _dot_nt(q, k_prev), -jnp.inf)
                m = jnp.maximum(jnp.max(s_cur, axis=-1, keepdims=True),
                                jnp.max(s_prev, axis=-1, keepdims=True))
                p_cur = jnp.exp(s_cur - m)
                p_prev = jnp.exp(s_prev - m)
                den = (jnp.sum(p_cur, axis=-1, keepdims=True)
                       + jnp.sum(p_prev, axis=-1, keepdims=True))
                acc = _dot(p_cur.astype(BF16), v_cur) + _dot(p_prev.astype(BF16), v_prev)
            else:
                m = jnp.max(s_cur, axis=-1, keepdims=True)
                p_cur = jnp.exp(s_cur - m)
                den = jnp.sum(p_cur, axis=-1, keepdims=True)
                acc = _dot(p_cur.astype(BF16), v_cur)
            o_ref[r0:r0 + WIN_BLOCK, cs] = (acc / den).astype(BF16)
            lse_ref[r0:r0 + WIN_BLOCK, hh:hh + 1] = m + jnp.log(den)


def _window_attention(proj, *, bsz, seq, group, d_model, rb=512):
    dil, rows_per_slab, ncols = proj.shape
    tokens = dil * rows_per_slab
    sub_len = seq // dil
    p2 = proj.reshape(tokens, ncols)
    rb = min(rb, rows_per_slab)
    blocks_per_slab = rows_per_slab // rb
    hb = rb // WIN_BLOCK
    t0 = _gate_tiles(group, d_model)
    prev_map = lambda c: (lambda i: (jnp.maximum(i * hb - 1, 0), c))
    out_map = lambda i: (i % blocks_per_slab, i // blocks_per_slab)
    kern = functools.partial(_window_kernel, sub_len=sub_len)
    o, lse = pl.pallas_call(
        kern,
        grid=(tokens // rb,),
        in_specs=[
            pl.BlockSpec((rb, GROUP_W), lambda i: (i, t0)),
            pl.BlockSpec((rb, GROUP_W), lambda i: (i, t0 + 1)),
            pl.BlockSpec((rb, GROUP_W), lambda i: (i, t0 + 2)),
            pl.BlockSpec((WIN_BLOCK, GROUP_W), prev_map(t0 + 1)),
            pl.BlockSpec((WIN_BLOCK, GROUP_W), prev_map(t0 + 2)),
        ],
        out_specs=[
            pl.BlockSpec((rb, GROUP_W), out_map),
            pl.BlockSpec((rb, HEAD_DIM), out_map),
        ],
        out_shape=[
            jax.ShapeDtypeStruct((rows_per_slab, dil * GROUP_W), BF16),
            jax.ShapeDtypeStruct((rows_per_slab, dil * HEAD_DIM), F32),
        ],
        compiler_params=_params("parallel"),
        name=f"window_attention_g{group}",
    )(p2, p2, p2, p2, p2)
    return o.reshape(tokens, GROUP_W), lse.reshape(tokens, HEAD_DIM)


def _softplus(z):
    return jnp.maximum(z, 0.0) + jnp.log(1.0 + jnp.exp(-jnp.abs(z)))


def _sb_kernel(q_ref, k_ref, v_ref, o_ref, *, tile):
    qi = pl.program_id(2)
    q = q_ref[...]
    row = lax.broadcasted_iota(jnp.int32, (tile, tile), 0)
    col = lax.broadcasted_iota(jnp.int32, (tile, tile), 1)
    suffix = (row > col).astype(BF16)
    causal = col < row

    def visit(kblk, acc, later, masked):
        k = k_ref[pl.ds(kblk * tile, tile), :]
        v = v_ref[pl.ds(kblk * tile, tile), :]
        z = _dot_nt(q, k)
        sp = _softplus(z)
        log_beta = z - sp
        log_1mb = -sp
        if masked:
            log_1mb = jnp.where(causal, log_1mb, 0.0)
        hi = log_1mb.astype(BF16)
        lo = (log_1mb - hi.astype(F32)).astype(BF16)
        after = _dot(hi, suffix) + _dot(lo, suffix) + later
        a = jnp.exp(log_beta + after)
        if masked:
            a = jnp.where(causal, a, 0.0)
        acc = acc + _dot(a.astype(BF16), v)
        later = later + jnp.sum(log_1mb, axis=-1, keepdims=True)
        return acc, later

    acc0 = jnp.zeros((tile, HEAD_DIM), F32)
    later0 = jnp.zeros((tile, 1), F32)
    acc, later = visit(qi, acc0, later0, True)

    def body(t, carry):
        return visit(qi - 1 - t, carry[0], carry[1], False)

    acc, later = lax.fori_loop(0, qi, body, (acc, later))
    o_ref[...] = acc.astype(BF16)


def _stick_breaking(proj0, *, bsz, seq, d_model, tile=256):
    tokens = proj0.shape[0]
    nq = seq // tile
    heads = GROUP_W // HEAD_DIM
    t0 = _gate_tiles(0, d_model) + 3
    kern = functools.partial(_sb_kernel, tile=tile)
    return pl.pallas_call(
        kern,
        grid=(bsz, heads, nq),
        in_specs=[
            pl.BlockSpec((tile, HEAD_DIM), lambda b, h, i: (b * nq + i, t0 * heads + h)),
            pl.BlockSpec((seq, HEAD_DIM), lambda b, h, i: (b, (t0 + 1) * heads + h)),
            pl.BlockSpec((seq, HEAD_DIM), lambda b, h, i: (b, (t0 + 2) * heads + h)),
        ],
        out_specs=pl.BlockSpec((tile, HEAD_DIM), lambda b, h, i: (b * nq + i, h)),
        out_shape=jax.ShapeDtypeStruct((tokens, GROUP_W), BF16),
        compiler_params=_params("parallel", "parallel", "arbitrary"),
        name="stick_breaking",
    )(proj0, proj0, proj0)


def _merge_kernel(h_ref, o0_ref, o1_ref, o2_ref, l0_ref, l1_ref, l2_ref, ob_ref,
                  sga_ref, sgb_ref, gate_ref, wa_ref, wb_ref, wo_ref, out_ref, oa_ref):
    for hh in range(HEADS_PER_GROUP):
        cs = slice(hh * HEAD_DIM, (hh + 1) * HEAD_DIM)
        l0 = l0_ref[:, hh:hh + 1]
        l1 = l1_ref[:, hh:hh + 1]
        l2 = l2_ref[:, hh:hh + 1]
        m = jnp.maximum(jnp.maximum(l0, l1), l2)
        e0 = jnp.exp(l0 - m)
        e1 = jnp.exp(l1 - m)
        e2 = jnp.exp(l2 - m)
        inv = 1.0 / (e0 + e1 + e2)
        oa = ((e0 * inv) * o0_ref[:, cs].astype(F32)
              + (e1 * inv) * o1_ref[:, cs].astype(F32)
              + (e2 * inv) * o2_ref[:, cs].astype(F32))
        oa_ref[:, cs] = oa.astype(BF16)
    y_a = _dot(oa_ref[...], wa_ref[...])
    y_b = _dot(ob_ref[...], wb_ref[...])
    merged = sga_ref[...].astype(F32) * y_a + sgb_ref[...].astype(F32) * y_b
    out_ref[...] = h_ref[...] + gate_ref[...] * _dot(merged.astype(BF16), wo_ref[...])


def _merge(h, o_groups, lse_groups, o_b, proj0, mod3, gate_idx, wa, wb, wo, *, seq, tm=256):
    tokens, d_model = h.shape
    blocks_per_seq = seq // tm
    row = lambda i: (i, 0)
    const = lambda i: (0, 0)
    resident = dict(pipeline_mode=pl.Buffered(1))
    return pl.pallas_call(
        _merge_kernel,
        grid=(tokens // tm,),
        in_specs=[
            pl.BlockSpec((tm, d_model), row),
            pl.BlockSpec((tm, GROUP_W), row),
            pl.BlockSpec((tm, GROUP_W), row),
            pl.BlockSpec((tm, GROUP_W), row),
            pl.BlockSpec((tm, HEAD_DIM), row),
            pl.BlockSpec((tm, HEAD_DIM), row),
            pl.BlockSpec((tm, HEAD_DIM), row),
            pl.BlockSpec((tm, GROUP_W), row),
            pl.BlockSpec((tm, d_model), lambda i: (i, 0)),
            pl.BlockSpec((tm, d_model), lambda i: (i, 1)),
            pl.BlockSpec((None, 1, d_model), lambda i: (i // blocks_per_seq, 0, gate_idx)),
            pl.BlockSpec(wa.shape, const, **resident),
            pl.BlockSpec(wb.shape, const, **resident),
            pl.BlockSpec(wo.shape, const, **resident),
        ],
        out_specs=pl.BlockSpec((tm, d_model), row),
        out_shape=jax.ShapeDtypeStruct((tokens, d_model), F32),
        scratch_shapes=[pltpu.VMEM((tm, GROUP_W), BF16)],
        compiler_params=_params("parallel"),
        name="merge",
    )(h, *o_groups, *lse_groups, o_b, proj0, proj0, mod3, wa, wb, wo)


def _ffn_kernel(h_ref, g_ref, scale_ref, shift_ref, gate_ref, wg_ref, wu_ref, wd_ref,
                out_ref, u_ref):
    f = pl.program_id(1)

    @pl.when(f == 0)
    def _():
        u_ref[...] = _modulated_norm(h_ref[...], g_ref[...], scale_ref[...],
                                     shift_ref[...]).astype(BF16)

    u = u_ref[...]
    gate_act = _dot(u, wg_ref[...])
    up = _dot(u, wu_ref[...])
    act = (gate_act * jax.nn.sigmoid(gate_act) * up).astype(BF16)
    part = _dot(act, wd_ref[...])

    @pl.when(f == 0)
    def _():
        out_ref[...] = part

    @pl.when(f > 0)
    def _():
        out_ref[...] += part

    @pl.when(f == pl.num_programs(1) - 1)
    def _():
        out_ref[...] = h_ref[...] + gate_ref[...] * out_ref[...]


def _ffn(h, g, mod3, shift_idx, scale_idx, gate_idx, w_gate_up, w_down, *, seq, tm=512, tf=512):
    tokens, d_model = h.shape
    d_ff = w_down.shape[0]
    nf = d_ff // tf
    blocks_per_seq = seq // tm
    mod_spec = lambda idx: pl.BlockSpec((None, 1, d_model),
                                        lambda i, f: (i // blocks_per_seq, 0, idx))
    return pl.pallas_call(
        _ffn_kernel,
        grid=(tokens // tm, nf),
        in_specs=[
            pl.BlockSpec((tm, d_model), lambda i, f: (i, 0)),
            pl.BlockSpec((1, d_model), lambda i, f: (0, 0)),
            mod_spec(scale_idx),
            mod_spec(shift_idx),
            mod_spec(gate_idx),
            pl.BlockSpec((d_model, tf), lambda i, f: (0, f)),
            pl.BlockSpec((d_model, tf), lambda i, f: (0, nf + f)),
            pl.BlockSpec((tf, d_model), lambda i, f: (f, 0)),
        ],
        out_specs=pl.BlockSpec((tm, d_model), lambda i, f: (i, 0)),
        out_shape=jax.ShapeDtypeStruct((tokens, d_model), F32),
        scratch_shapes=[pltpu.VMEM((tm, d_model), BF16)],
        compiler_params=_params("parallel", "arbitrary"),
        name="ffn",
    )(h, g, mod3, mod3, mod3, w_gate_up, w_gate_up, w_down)


def _rope_tables(seq):
    inv = jnp.power(ROPE_THETA, -jnp.arange(0, HEAD_DIM, 2, dtype=F32) / HEAD_DIM)
    ang = jnp.arange(seq, dtype=F32)[:, None] * inv[None, :]
    cos, sin = jnp.cos(ang), jnp.sin(ang)
    return jnp.concatenate([cos, cos], axis=-1), jnp.concatenate([-sin, sin], axis=-1)


def kernel(x, c, w_ada, b_ada, norm1_g, norm2_g, w_in, qn_g, kn_g, w_branch_a, w_branch_b,
           w_out, w_gate_up, w_down):
    bsz, seq, d_model = x.shape
    depth = w_ada.shape[0]
    tokens = bsz * seq
    cos_full, sin_full = _rope_tables(seq)
    mod = _modulation(c, w_ada, b_ada)
    h = x.reshape(tokens, d_model)
    for l in range(depth):
        mod3 = mod[l].reshape(bsz, 1, 6 * d_model)
        u = _norm_modulate(h, norm1_g[l].reshape(1, d_model), mod3, 0, 1, seq)
        w_in_l = w_in[l].astype(BF16)
        qg = qn_g[l].reshape(1, HEAD_DIM)
        kg = kn_g[l].reshape(1, HEAD_DIM)
        projs = [_input_projection(u, w_in_l, cos_full, sin_full, qg, kg,
                                   bsz=bsz, seq=seq, group=g) for g in range(len(DIL_PATTERNS))]
        outs = [_window_attention(projs[g], bsz=bsz, seq=seq, group=g, d_model=d_model)
                for g in range(len(DIL_PATTERNS))]
        proj0 = projs[0].reshape(tokens, -1)
        o_b = _stick_breaking(proj0, bsz=bsz, seq=seq, d_model=d_model)
        h = _merge(h, [o for o, _ in outs], [s for _, s in outs], o_b, proj0, mod3, 2,
                   w_branch_a[l].astype(BF16), w_branch_b[l].astype(BF16),
                   w_out[l].astype(BF16), seq=seq)
        h = _ffn(h, norm2_g[l].reshape(1, d_model), mod3, 3, 4, 5,
                 w_gate_up[l].astype(BF16), w_down[l].astype(BF16), seq=seq)
    return h.reshape(bsz, seq, d_model)
```

```python
import functools

import jax
import jax.numpy as jnp
from jax import lax
from jax.experimental import pallas as pl
from jax.experimental.pallas import tpu as pltpu

F32 = jnp.float32
BF16 = jnp.bfloat16

HEAD_DIM = 128
HEADS_PER_GROUP = 4
GROUP_W = HEADS_PER_GROUP * HEAD_DIM
DIL_PATTERNS = ((128, 1), (512, 4), (2048, 16))
WIN_BLOCK = 128
ROPE_THETA = 10000.0
EPS = 1e-6
QK_SCALE = HEAD_DIM ** -0.5

VMEM_LIMIT_BYTES = 56 * 1024 * 1024


def _params(*semantics):
    return pltpu.CompilerParams(dimension_semantics=semantics,
                                vmem_limit_bytes=VMEM_LIMIT_BYTES)


def _dot(a, b):
    return jnp.dot(a, b, preferred_element_type=F32)


def _dot_nt(a, b):
    return lax.dot_general(a, b, (((1,), (1,)), ((), ())), preferred_element_type=F32)


def _mod_kernel(c_ref, w_ref, b_ref, o_ref):
    c = c_ref[...]
    c_act = (c * jax.nn.sigmoid(c)).astype(BF16)
    o_ref[...] = _dot(c_act, w_ref[...].astype(BF16)) + b_ref[...]


def _modulation(c, w_ada, b_ada, tn=512):
    depth, d_model, n = w_ada.shape
    bsz = c.shape[0]
    return pl.pallas_call(
        _mod_kernel,
        grid=(depth, n // tn),
        in_specs=[
            pl.BlockSpec((bsz, d_model), lambda l, j: (0, 0)),
            pl.BlockSpec((None, d_model, tn), lambda l, j: (l, 0, j)),
            pl.BlockSpec((None, 1, tn), lambda l, j: (l, 0, j)),
        ],
        out_specs=pl.BlockSpec((None, bsz, tn), lambda l, j: (l, 0, j)),
        out_shape=jax.ShapeDtypeStruct((depth, bsz, n), F32),
        compiler_params=_params("parallel", "parallel"),
        name="modulation",
    )(c, w_ada, b_ada.reshape(depth, 1, n))


def _modulated_norm(h, g, scale, shift):
    ms = jnp.mean(h * h, axis=-1, keepdims=True)
    return (h * lax.rsqrt(ms + EPS)) * g * (1.0 + scale) + shift


def _norm_kernel(h_ref, g_ref, scale_ref, shift_ref, o_ref):
    o_ref[...] = _modulated_norm(h_ref[...], g_ref[...], scale_ref[...],
                                 shift_ref[...]).astype(BF16)


def _norm_modulate(h, g, mod3, shift_idx, scale_idx, seq, tm=512):
    tokens, d_model = h.shape
    blocks_per_seq = seq // tm
    return pl.pallas_call(
        _norm_kernel,
        grid=(tokens // tm,),
        in_specs=[
            pl.BlockSpec((tm, d_model), lambda i: (i, 0)),
            pl.BlockSpec((1, d_model), lambda i: (0, 0)),
            pl.BlockSpec((None, 1, d_model), lambda i: (i // blocks_per_seq, 0, scale_idx)),
            pl.BlockSpec((None, 1, d_model), lambda i: (i // blocks_per_seq, 0, shift_idx)),
        ],
        out_specs=pl.BlockSpec((tm, d_model), lambda i: (i, 0)),
        out_shape=jax.ShapeDtypeStruct((tokens, d_model), BF16),
        compiler_params=_params("parallel"),
        name="norm_modulate",
    )(h, g, mod3, mod3)


def _qk_epilogue(acc, gain, cos_ref, sin_ref, o_ref):
    tm = acc.shape[0]
    rows = cos_ref.shape[0]
    cos = cos_ref[...]
    sin = sin_ref[...]
    for seg in range(tm // rows):
        r0 = seg * rows
        for hh in range(HEADS_PER_GROUP):
            c0 = hh * HEAD_DIM
            x = acc[r0:r0 + rows, c0:c0 + HEAD_DIM]
            ms = jnp.mean(x * x, axis=-1, keepdims=True)
            y = (x * lax.rsqrt(ms + EPS)) * gain
            out = y * cos + pltpu.roll(y, HEAD_DIM // 2, 1) * sin
            o_ref[r0:r0 + rows, c0:c0 + HEAD_DIM] = out.astype(BF16)


def _proj_kernel(u_ref, w_ref, cos_ref, sin_ref, qg_ref, kg_ref, o_ref, *, n_gate):
    j = pl.program_id(2)
    acc = _dot(u_ref[...], w_ref[...])

    if n_gate:
        @pl.when(j < n_gate)
        def _():
            o_ref[...] = jax.nn.sigmoid(acc).astype(BF16)

    @pl.when(j == n_gate)
    def _():
        _qk_epilogue(acc, qg_ref[...] * QK_SCALE, cos_ref, sin_ref, o_ref)

    @pl.when(j == n_gate + 1)
    def _():
        _qk_epilogue(acc, kg_ref[...], cos_ref, sin_ref, o_ref)

    if n_gate:
        @pl.when(j == n_gate + 3)
        def _():
            o_ref[...] = (acc * QK_SCALE).astype(BF16)

        @pl.when((j == n_gate + 2) | (j > n_gate + 3))
        def _():
            o_ref[...] = acc.astype(BF16)
    else:
        @pl.when(j >= n_gate + 2)
        def _():
            o_ref[...] = acc.astype(BF16)


def _gate_tiles(group, d_model):
    return 2 * d_model // GROUP_W if group == 0 else 0


def _input_projection(u, w, layer, cos_full, sin_full, qn_g, kn_g, *, bsz, seq, group, tm=1024):
    tokens, d_model = u.shape
    dil = DIL_PATTERNS[group][1]
    sub_len = seq // dil
    rows_total = tokens // dil
    tm = min(tm, rows_total)
    n_groups = len(DIL_PATTERNS)
    n_gate = _gate_tiles(group, d_model)
    if group == 0:
        n_tiles = n_gate + 6
        a_tiles = 3 * n_groups

        def col_of(j):
            jj = j - n_gate
            qkv0_or_b = jnp.where(jj < 3, n_groups * jj, jj - 3 + a_tiles)
            return jnp.where(j < n_gate, j + a_tiles + 3, qkv0_or_b)
    else:
        n_tiles = 3
        col_of = lambda j: n_groups * j + group
    rope_rows = min(tm, sub_len)
    rope_blocks = sub_len // rope_rows
    kern = functools.partial(_proj_kernel, n_gate=n_gate)
    return pl.pallas_call(
        kern,
        grid=(dil, rows_total // tm, n_tiles),
        in_specs=[
            pl.BlockSpec((tm, d_model), lambda r, i, j: (i, r)),
            pl.BlockSpec((None, d_model, GROUP_W), lambda r, i, j: (layer, 0, col_of(j))),
            pl.BlockSpec((rope_rows, HEAD_DIM), lambda r, i, j: (i % rope_blocks, r)),
            pl.BlockSpec((rope_rows, HEAD_DIM), lambda r, i, j: (i % rope_blocks, r)),
            pl.BlockSpec((1, HEAD_DIM), lambda r, i, j: (0, 0)),
            pl.BlockSpec((1, HEAD_DIM), lambda r, i, j: (0, 0)),
        ],
        out_specs=pl.BlockSpec((None, tm, GROUP_W), lambda r, i, j: (r, i, j)),
        out_shape=jax.ShapeDtypeStruct((dil, rows_total, n_tiles * GROUP_W), BF16),
        compiler_params=_params("parallel", "parallel", "arbitrary"),
        name=f"input_projection_g{group}",
    )(u.reshape(rows_total, dil * d_model), w,
      cos_full.reshape(sub_len, dil * HEAD_DIM), sin_full.reshape(sub_len, dil * HEAD_DIM),
      qn_g, kn_g)


def _window_kernel(q_ref, k_ref, v_ref, kp_ref, vp_ref, o_ref, lse_ref, *, sub_len):
    rb = q_ref.shape[0]
    blocks_per_sub = max(sub_len // rb, 1)
    starts_sub = (pl.program_id(0) % blocks_per_sub) == 0
    qi_ = lax.broadcasted_iota(jnp.int32, (WIN_BLOCK, WIN_BLOCK), 0)
    kj_ = lax.broadcasted_iota(jnp.int32, (WIN_BLOCK, WIN_BLOCK), 1)
    cur_mask = kj_ <= qi_
    prev_mask = kj_ >= qi_
    halo_mask = kj_ >= qi_ + jnp.where(starts_sub, WIN_BLOCK, 0)
    lse_ref[...] = jnp.zeros(lse_ref.shape, F32)
    for qb in range(rb // WIN_BLOCK):
        r0 = qb * WIN_BLOCK
        first_in_sub = (r0 % sub_len) == 0
        for hh in range(HEADS_PER_GROUP):
            cs = slice(hh * HEAD_DIM, (hh + 1) * HEAD_DIM)
            q = q_ref[r0:r0 + WIN_BLOCK, cs]
            k_cur = k_ref[r0:r0 + WIN_BLOCK, cs]
            v_cur = v_ref[r0:r0 + WIN_BLOCK, cs]
            s_cur = jnp.where(cur_mask, _dot_nt(q, k_cur), -jnp.inf)
            use_prev = not (first_in_sub and sub_len <= rb)
            if use_prev:
                if qb == 0:
                    k_prev = kp_ref[:, cs]
                    v_prev = vp_ref[:, cs]
                    pmask = halo_mask
                else:
                    k_prev = k_ref[r0 - WIN_BLOCK:r0, cs]
                    v_prev = v_ref[r0 - WIN_BLOCK:r0, cs]
                    pmask = prev_mask
                s_prev = jnp.where(pmask, _dot_nt(q, k_prev), -jnp.inf)
                m = jnp.maximum(jnp.max(s_cur, axis=-1, keepdims=True),
                                jnp.max(s_prev, axis=-1, keepdims=True))
                p_cur = jnp.exp(s_cur - m)
                p_prev = jnp.exp(s_prev - m)
                den = (jnp.sum(p_cur, axis=-1, keepdims=True)
                       + jnp.sum(p_prev, axis=-1, keepdims=True))
                acc = _dot(p_cur.astype(BF16), v_cur) + _dot(p_prev.astype(BF16), v_prev)
            else:
                m = jnp.max(s_cur, axis=-1, keepdims=True)
                p_cur = jnp.exp(s_cur - m)
                den = jnp.sum(p_cur, axis=-1, keepdims=True)
                acc = _dot(p_cur.astype(BF16), v_cur)
            o_ref[r0:r0 + WIN_BLOCK, cs] = (acc / den).astype(BF16)
            lse_ref[r0:r0 + WIN_BLOCK, hh:hh + 1] = m + jnp.log(den)


def _window_attention(proj, *, bsz, seq, group, d_model, rb=512):
    dil, rows_per_slab, ncols = proj.shape
    tokens = dil * rows_per_slab
    sub_len = seq // dil
    p2 = proj.reshape(tokens, ncols)
    rb = min(rb, rows_per_slab)
    blocks_per_slab = rows_per_slab // rb
    hb = rb // WIN_BLOCK
    t0 = _gate_tiles(group, d_model)
    prev_map = lambda c: (lambda i: (jnp.maximum(i * hb - 1, 0), c))
    out_map = lambda i: (i % blocks_per_slab, i // blocks_per_slab)
    kern = functools.partial(_window_kernel, sub_len=sub_len)
    o, lse = pl.pallas_call(
        kern,
        grid=(tokens // rb,),
        in_specs=[
            pl.BlockSpec((rb, GROUP_W), lambda i: (i, t0)),
            pl.BlockSpec((rb, GROUP_W), lambda i: (i, t0 + 1)),
            pl.BlockSpec((rb, GROUP_W), lambda i: (i, t0 + 2)),
            pl.BlockSpec((WIN_BLOCK, GROUP_W), prev_map(t0 + 1)),
            pl.BlockSpec((WIN_BLOCK, GROUP_W), prev_map(t0 + 2)),
        ],
        out_specs=[
            pl.BlockSpec((rb, GROUP_W), out_map),
            pl.BlockSpec((rb, HEAD_DIM), out_map),
        ],
        out_shape=[
            jax.ShapeDtypeStruct((rows_per_slab, dil * GROUP_W), BF16),
            jax.ShapeDtypeStruct((rows_per_slab, dil * HEAD_DIM), F32),
        ],
        compiler_params=_params("parallel"),
        name=f"window_attention_g{group}",
    )(p2, p2, p2, p2, p2)
    return o.reshape(tokens, GROUP_W), lse.reshape(tokens, HEAD_DIM)


def _softplus(z):
    return jnp.maximum(z, 0.0) + jnp.log(1.0 + jnp.exp(-jnp.abs(z)))


def _sb_kernel(q_ref, k_ref, v_ref, o_ref, *, tq, tk):
    qi = pl.program_id(1)
    heads = q_ref.shape[1] // HEAD_DIM
    diag_blocks = tq // tk
    m = heads * tq
    head_cols = [slice(hh * HEAD_DIM, (hh + 1) * HEAD_DIM) for hh in range(heads)]
    head_rows = [slice(hh * tq, (hh + 1) * tq) for hh in range(heads)]
    r = lax.broadcasted_iota(jnp.int32, (tk, tk), 0)
    c = lax.broadcasted_iota(jnp.int32, (tk, tk), 1)
    suffix = (r > c).astype(BF16)
    q_row = lax.broadcasted_iota(jnp.int32, (m, tk), 0) % tq
    k_col = lax.broadcasted_iota(jnp.int32, (m, tk), 1)

    def visit(kblk, acc, later, diag_offset):
        rows = pl.ds(pl.multiple_of(kblk * tk, tk), tk)
        z = jnp.concatenate([_dot_nt(q_ref[:, cs], k_ref[rows, cs]) for cs in head_cols],
                            axis=0)
        sp = _softplus(z)
        log_beta = z - sp
        log_1mb = -sp
        if diag_offset is not None:
            causal = k_col + diag_offset < q_row
            log_1mb = jnp.where(causal, log_1mb, 0.0)
        hi = log_1mb.astype(BF16)
        lo = (log_1mb - hi.astype(F32)).astype(BF16)
        after = _dot(hi, suffix) + _dot(lo, suffix) + later
        a = jnp.exp(log_beta + after)
        if diag_offset is not None:
            a = jnp.where(causal, a, 0.0)
        a = a.astype(BF16)
        pv = jnp.concatenate([_dot(a[rs], v_ref[rows, cs])
                              for rs, cs in zip(head_rows, head_cols)], axis=0)
        return acc + pv, later + jnp.sum(log_1mb, axis=-1, keepdims=True)

    acc = jnp.zeros((m, HEAD_DIM), F32)
    later = jnp.zeros((m, 1), F32)
    for d in reversed(range(diag_blocks)):
        acc, later = visit(qi * diag_blocks + d, acc, later, d * tk)
    acc, later = lax.fori_loop(
        0, qi * diag_blocks,
        lambda t, carry: visit(qi * diag_blocks - 1 - t, carry[0], carry[1], None),
        (acc, later))
    for rs, cs in zip(head_rows, head_cols):
        o_ref[:, cs] = acc[rs].astype(BF16)


def _stick_breaking(proj0, *, bsz, seq, d_model, tq=512, tk=256):
    tokens = proj0.shape[0]
    nq = seq // tq
    t0 = _gate_tiles(0, d_model) + 3
    kern = functools.partial(_sb_kernel, tq=tq, tk=tk)
    return pl.pallas_call(
        kern,
        grid=(bsz, nq),
        in_specs=[
            pl.BlockSpec((tq, GROUP_W), lambda b, i: (b * nq + i, t0)),
            pl.BlockSpec((seq, GROUP_W), lambda b, i: (b, t0 + 1)),
            pl.BlockSpec((seq, GROUP_W), lambda b, i: (b, t0 + 2)),
        ],
        out_specs=pl.BlockSpec((tq, GROUP_W), lambda b, i: (b * nq + i, 0)),
        out_shape=jax.ShapeDtypeStruct((tokens, GROUP_W), BF16),
        compiler_params=_params("parallel", "arbitrary"),
        name="stick_breaking",
    )(proj0, proj0, proj0)


def _merge_kernel(h_ref, o0_ref, o1_ref, o2_ref, l0_ref, l1_ref, l2_ref, ob_ref,
                  sga_ref, sgb_ref, gate_ref, wa_ref, wb_ref, wo_ref, out_ref, oa_ref):
    for hh in range(HEADS_PER_GROUP):
        cs = slice(hh * HEAD_DIM, (hh + 1) * HEAD_DIM)
        l0 = l0_ref[:, hh:hh + 1]
        l1 = l1_ref[:, hh:hh + 1]
        l2 = l2_ref[:, hh:hh + 1]
        m = jnp.maximum(jnp.maximum(l0, l1), l2)
        e0 = jnp.exp(l0 - m)
        e1 = jnp.exp(l1 - m)
        e2 = jnp.exp(l2 - m)
        inv = 1.0 / (e0 + e1 + e2)
        oa = ((e0 * inv) * o0_ref[:, cs].astype(F32)
              + (e1 * inv) * o1_ref[:, cs].astype(F32)
              + (e2 * inv) * o2_ref[:, cs].astype(F32))
        oa_ref[:, cs] = oa.astype(BF16)
    y_a = _dot(oa_ref[...], wa_ref[...])
    y_b = _dot(ob_ref[...], wb_ref[...])
    merged = sga_ref[...].astype(F32) * y_a + sgb_ref[...].astype(F32) * y_b
    out_ref[...] = h_ref[...] + gate_ref[...] * _dot(merged.astype(BF16), wo_ref[...])


def _merge(h, o_groups, lse_groups, o_b, proj0, mod3, gate_idx, wa, wb, wo, layer, *, seq,
           tm=256):
    tokens, d_model = h.shape
    blocks_per_seq = seq // tm
    row = lambda i: (i, 0)
    resident = lambda w: pl.BlockSpec((None,) + w.shape[1:], lambda i: (layer, 0, 0),
                                      pipeline_mode=pl.Buffered(1))
    return pl.pallas_call(
        _merge_kernel,
        grid=(tokens // tm,),
        in_specs=[
            pl.BlockSpec((tm, d_model), row),
            pl.BlockSpec((tm, GROUP_W), row),
            pl.BlockSpec((tm, GROUP_W), row),
            pl.BlockSpec((tm, GROUP_W), row),
            pl.BlockSpec((tm, HEAD_DIM), row),
            pl.BlockSpec((tm, HEAD_DIM), row),
            pl.BlockSpec((tm, HEAD_DIM), row),
            pl.BlockSpec((tm, GROUP_W), row),
            pl.BlockSpec((tm, d_model), lambda i: (i, 0)),
            pl.BlockSpec((tm, d_model), lambda i: (i, 1)),
            pl.BlockSpec((None, 1, d_model), lambda i: (i // blocks_per_seq, 0, gate_idx)),
            resident(wa),
            resident(wb),
            resident(wo),
        ],
        out_specs=pl.BlockSpec((tm, d_model), row),
        out_shape=jax.ShapeDtypeStruct((tokens, d_model), F32),
        scratch_shapes=[pltpu.VMEM((tm, GROUP_W), BF16)],
        compiler_params=_params("parallel"),
        name="merge",
    )(h, *o_groups, *lse_groups, o_b, proj0, proj0, mod3, wa, wb, wo)


def _ffn_kernel(h_ref, g_ref, scale_ref, shift_ref, hres_ref, gate_ref, wg_ref, wu_ref, wd_ref,
                out_ref, u_ref, act_ref, *, nf, tf):
    s = pl.program_id(1)

    @pl.when(s == 0)
    def _():
        u_ref[...] = _modulated_norm(h_ref[...], g_ref[...], scale_ref[...],
                                     shift_ref[...]).astype(BF16)

    @pl.when(s < nf)
    def _():
        u = u_ref[...]
        gate_act = _dot(u, wg_ref[...])
        up = _dot(u, wu_ref[...])
        cols = pl.ds(pl.multiple_of(s * tf, tf), tf)
        act_ref[:, cols] = (gate_act * jax.nn.sigmoid(gate_act) * up).astype(BF16)

    @pl.when(s >= nf)
    def _():
        out_ref[...] = hres_ref[...] + gate_ref[...] * _dot(act_ref[...], wd_ref[...])


def _ffn(h, g, mod3, shift_idx, scale_idx, gate_idx, w_gate_up, w_down, layer, *, seq,
         tm=1024, tf=512, tn=256):
    tokens, d_model = h.shape
    d_ff = w_down.shape[1]
    nf = d_ff // tf
    nd = d_model // tn
    blocks_per_seq = seq // tm
    batch_of = lambda i: i // blocks_per_seq
    up_step = lambda s: jnp.minimum(s, nf - 1)
    down_step = lambda s: jnp.clip(s - nf, 0, nd - 1)
    mod_spec = lambda idx: pl.BlockSpec((None, 1, d_model), lambda i, s: (batch_of(i), 0, idx))
    kern = functools.partial(_ffn_kernel, nf=nf, tf=tf)
    return pl.pallas_call(
        kern,
        grid=(tokens // tm, nf + nd),
        in_specs=[
            pl.BlockSpec((tm, d_model), lambda i, s: (i, 0), pipeline_mode=pl.Buffered(1)),
            pl.BlockSpec((1, d_model), lambda i, s: (0, 0)),
            mod_spec(scale_idx),
            mod_spec(shift_idx),
            pl.BlockSpec((tm, tn), lambda i, s: (i, down_step(s))),
            pl.BlockSpec((None, 1, tn),
                         lambda i, s: (batch_of(i), 0, gate_idx * nd + down_step(s))),
            pl.BlockSpec((None, d_model, tf), lambda i, s: (layer, 0, up_step(s))),
            pl.BlockSpec((None, d_model, tf), lambda i, s: (layer, 0, nf + up_step(s))),
            pl.BlockSpec((None, d_ff, tn), lambda i, s: (layer, 0, down_step(s))),
        ],
        out_specs=pl.BlockSpec((tm, tn), lambda i, s: (i, down_step(s))),
        out_shape=jax.ShapeDtypeStruct((tokens, d_model), F32),
        scratch_shapes=[pltpu.VMEM((tm, d_model), BF16), pltpu.VMEM((tm, d_ff), BF16)],
        compiler_params=_params("parallel", "arbitrary"),
        name="ffn",
    )(h, g, mod3, mod3, h, mod3, w_gate_up, w_gate_up, w_down)


def _rope_tables(seq):
    inv = jnp.power(ROPE_THETA, -jnp.arange(0, HEAD_DIM, 2, dtype=F32) / HEAD_DIM)
    ang = jnp.arange(seq, dtype=F32)[:, None] * inv[None, :]
    cos, sin = jnp.cos(ang), jnp.sin(ang)
    return jnp.concatenate([cos, cos], axis=-1), jnp.concatenate([-sin, sin], axis=-1)


def kernel(x, c, w_ada, b_ada, norm1_g, norm2_g, w_in, qn_g, kn_g, w_branch_a, w_branch_b,
           w_out, w_gate_up, w_down):
    bsz, seq, d_model = x.shape
    depth = w_ada.shape[0]
    tokens = bsz * seq
    cos_full, sin_full = _rope_tables(seq)
    mod = _modulation(c, w_ada, b_ada)
    w_in, w_branch_a, w_branch_b, w_out, w_gate_up, w_down = (
        w.astype(BF16) for w in (w_in, w_branch_a, w_branch_b, w_out, w_gate_up, w_down))
    h = x.reshape(tokens, d_model)
    for l in range(depth):
        mod3 = mod[l].reshape(bsz, 1, 6 * d_model)
        u = _norm_modulate(h, norm1_g[l].reshape(1, d_model), mod3, 0, 1, seq)
        qg = qn_g[l].reshape(1, HEAD_DIM)
        kg = kn_g[l].reshape(1, HEAD_DIM)
        projs = [_input_projection(u, w_in, l, cos_full, sin_full, qg, kg,
                                   bsz=bsz, seq=seq, group=g) for g in range(len(DIL_PATTERNS))]
        outs = [_window_attention(projs[g], bsz=bsz, seq=seq, group=g, d_model=d_model)
                for g in range(len(DIL_PATTERNS))]
        proj0 = projs[0].reshape(tokens, -1)
        o_b = _stick_breaking(proj0, bsz=bsz, seq=seq, d_model=d_model)
        h = _merge(h, [o for o, _ in outs], [s for _, s in outs], o_b, proj0, mod3, 2,
                   w_branch_a, w_branch_b, w_out, l, seq=seq)
        h = _ffn(h, norm2_g[l].reshape(1, d_model), mod3, 3, 4, 5, w_gate_up, w_down, l, seq=seq)
    return h.reshape(bsz, seq, d_model)
```

```python
import functools

import jax
import jax.numpy as jnp
from jax import lax
from jax.experimental import pallas as pl
from jax.experimental.pallas import tpu as pltpu

F32 = jnp.float32
BF16 = jnp.bfloat16

HEAD_DIM = 128
HEADS_PER_GROUP = 4
GROUP_W = HEADS_PER_GROUP * HEAD_DIM
DIL_PATTERNS = ((128, 1), (512, 4), (2048, 16))
N_GROUPS = len(DIL_PATTERNS)
WIN_BLOCK = 128
PERM_BLOCK = 256
ROPE_THETA = 10000.0
EPS = 1e-6
QK_SCALE = HEAD_DIM ** -0.5

VMEM_LIMIT_BYTES = 56 * 1024 * 1024


def _params(*semantics):
    return pltpu.CompilerParams(dimension_semantics=semantics,
                                vmem_limit_bytes=VMEM_LIMIT_BYTES)


def _dot(a, b):
    return jnp.dot(a, b, preferred_element_type=F32)


def _dot_nt(a, b):
    return lax.dot_general(a, b, (((1,), (1,)), ((), ())), preferred_element_type=F32)


def _mod_kernel(c_ref, w_ref, b_ref, o_ref):
    c = c_ref[...]
    c_act = (c * jax.nn.sigmoid(c)).astype(BF16)
    o_ref[...] = _dot(c_act, w_ref[...].astype(BF16)) + b_ref[...]


def _modulation(c, w_ada, b_ada, tn=512):
    depth, d_model, n = w_ada.shape
    bsz = c.shape[0]
    return pl.pallas_call(
        _mod_kernel,
        grid=(depth, n // tn),
        in_specs=[
            pl.BlockSpec((bsz, d_model), lambda l, j: (0, 0)),
            pl.BlockSpec((None, d_model, tn), lambda l, j: (l, 0, j)),
            pl.BlockSpec((None, 1, tn), lambda l, j: (l, 0, j)),
        ],
        out_specs=pl.BlockSpec((None, bsz, tn), lambda l, j: (l, 0, j)),
        out_shape=jax.ShapeDtypeStruct((depth, bsz, n), F32),
        compiler_params=_params("parallel", "parallel"),
        name="modulation",
    )(c, w_ada, b_ada.reshape(depth, 1, n))


def _modulated_norm(h, g, scale, shift):
    ms = jnp.mean(h * h, axis=-1, keepdims=True)
    return (h * lax.rsqrt(ms + EPS)) * g * (1.0 + scale) + shift


def _norm_kernel(h_ref, g_ref, scale_ref, shift_ref, o_ref):
    o_ref[...] = _modulated_norm(h_ref[...], g_ref[...], scale_ref[...],
                                 shift_ref[...]).astype(BF16)


def _norm_modulate(h, g, mod3, shift_idx, scale_idx, seq, tm=512):
    tokens, d_model = h.shape
    blocks_per_seq = seq // tm
    return pl.pallas_call(
        _norm_kernel,
        grid=(tokens // tm,),
        in_specs=[
            pl.BlockSpec((tm, d_model), lambda i: (i, 0)),
            pl.BlockSpec((1, d_model), lambda i: (0, 0)),
            pl.BlockSpec((None, 1, d_model), lambda i: (i // blocks_per_seq, 0, scale_idx)),
            pl.BlockSpec((None, 1, d_model), lambda i: (i // blocks_per_seq, 0, shift_idx)),
        ],
        out_specs=pl.BlockSpec((tm, d_model), lambda i: (i, 0)),
        out_shape=jax.ShapeDtypeStruct((tokens, d_model), BF16),
        compiler_params=_params("parallel"),
        name="norm_modulate",
    )(h, g, mod3, mod3)


def _proj_rope_kernel(u_ref, w_ref, cos_ref, sin_ref, gain_ref, o_ref):
    acc = _dot(u_ref[...], w_ref[...])

    @pl.when(pl.program_id(1) >= 0)
    def _():
        cos = cos_ref[...]
        sin = sin_ref[...]
        gain = gain_ref[...]
        for hh in range(HEADS_PER_GROUP):
            cs = slice(hh * HEAD_DIM, (hh + 1) * HEAD_DIM)
            x = acc[:, cs]
            ms = jnp.mean(x * x, axis=-1, keepdims=True)
            y = (x * lax.rsqrt(ms + EPS)) * gain
            o_ref[:, cs] = (y * cos + pltpu.roll(y, HEAD_DIM // 2, 1) * sin).astype(BF16)


def _proj_scaled_kernel(u_ref, w_ref, mult_ref, o_ref):
    o_ref[...] = (_dot(u_ref[...], w_ref[...]) * mult_ref[...]).astype(BF16)


def _proj_sigmoid_kernel(u_ref, w_ref, o_ref):
    o_ref[...] = jax.nn.sigmoid(_dot(u_ref[...], w_ref[...])).astype(BF16)


def _projection(body, name, u, w, layer, first_tile, n_tiles, extra_specs, extra_args, tm=1024):
    tokens, d_model = u.shape
    return pl.pallas_call(
        body,
        grid=(tokens // tm, n_tiles),
        in_specs=[
            pl.BlockSpec((tm, d_model), lambda i, j: (i, 0)),
            pl.BlockSpec((None, d_model, GROUP_W), lambda i, j: (layer, 0, first_tile + j)),
        ] + extra_specs,
        out_specs=pl.BlockSpec((tm, GROUP_W), lambda i, j: (i, j)),
        out_shape=jax.ShapeDtypeStruct((tokens, n_tiles * GROUP_W), BF16),
        compiler_params=_params("parallel", "arbitrary"),
        name=name,
    )(u, w, *extra_args)


def _input_projections(u, w_in, layer, cos_full, sin_full, qn_g, kn_g, *, seq, tm=1024):
    d_model = u.shape[1]
    blocks_per_seq = seq // tm
    rope_spec = pl.BlockSpec((tm, HEAD_DIM), lambda i, j: (i % blocks_per_seq, 0))
    gains = jnp.stack([qn_g * QK_SCALE, kn_g]).reshape(2, 1, HEAD_DIM)
    qk = _projection(
        _proj_rope_kernel, "projection_qk", u, w_in, layer, 0, 2 * N_GROUPS,
        [rope_spec, rope_spec,
         pl.BlockSpec((None, 1, HEAD_DIM), lambda i, j: (j // N_GROUPS, 0, 0))],
        [cos_full, sin_full, gains], tm=tm)
    n_vb = N_GROUPS + 3
    mult = jnp.ones((n_vb, 1, GROUP_W), F32).at[N_GROUPS].set(QK_SCALE)
    vb = _projection(
        _proj_scaled_kernel, "projection_vb", u, w_in, layer, 2 * N_GROUPS, n_vb,
        [pl.BlockSpec((None, 1, GROUP_W), lambda i, j: (j, 0, 0))], [mult], tm=tm)
    gates = _projection(
        _proj_sigmoid_kernel, "projection_gates", u, w_in, layer, 2 * N_GROUPS + n_vb,
        2 * d_model // GROUP_W, [], [], tm=tm)
    return qk, vb, gates


def _residue_permutation(dil, transpose):
    n = PERM_BLOCK // dil
    a = lax.broadcasted_iota(jnp.int32, (PERM_BLOCK, PERM_BLOCK), 1 if transpose else 0)
    b = lax.broadcasted_iota(jnp.int32, (PERM_BLOCK, PERM_BLOCK), 0 if transpose else 1)
    return (b == (a % n) * dil + a // n).astype(BF16)


def _split3(x):
    h1 = x.astype(BF16).astype(F32)
    r1 = x - h1
    h2 = r1.astype(BF16).astype(F32)
    h3 = (r1 - h2).astype(BF16).astype(F32)
    return h1, h2, h3


def _dilated_kernel(q_ref, k_ref, v_ref, o_ref, lse_ref, qs_ref, ks_ref, vs_ref, os_ref, ls_ref,
                    *, dil, rows_per_batch):
    seq = q_ref.shape[0]
    sub_len = seq // dil
    n = PERM_BLOCK // dil
    pad = WIN_BLOCK

    zeros = jnp.zeros((pad, GROUP_W), BF16)
    ks_ref[0:pad, :] = zeros
    vs_ref[0:pad, :] = zeros
    if dil == 1:
        ks_ref[pad:, :] = k_ref[...]
        vs_ref[pad:, :] = v_ref[...]
        q_src = q_ref
    else:
        perm = _residue_permutation(dil, transpose=False)
        for src, dst, off in ((q_ref, qs_ref, 0), (k_ref, ks_ref, pad), (v_ref, vs_ref, pad)):
            for nb in range(seq // PERM_BLOCK):
                y = _dot(perm, src[nb * PERM_BLOCK:(nb + 1) * PERM_BLOCK, :]).astype(BF16)
                for r in range(dil):
                    d0 = off + r * sub_len + nb * n
                    dst[d0:d0 + n, :] = y[r * n:(r + 1) * n, :]
        q_src = qs_ref
    o_dst, l_dst = (o_ref, lse_ref) if dil == 1 else (os_ref, ls_ref)

    qi_ = lax.broadcasted_iota(jnp.int32, (WIN_BLOCK, 2 * WIN_BLOCK), 0)
    kj_ = lax.broadcasted_iota(jnp.int32, (WIN_BLOCK, 2 * WIN_BLOCK), 1)
    band = (kj_ >= qi_) & (kj_ <= qi_ + WIN_BLOCK)
    band_first = band & (kj_ >= WIN_BLOCK)
    lane = lax.broadcasted_iota(jnp.int32, (WIN_BLOCK, HEAD_DIM), 1)
    n_qb = rows_per_batch // WIN_BLOCK
    for b0 in range(0, seq, rows_per_batch):
        units = [(b0 + qb * WIN_BLOCK, hh) for qb in range(n_qb) for hh in range(HEADS_PER_GROUP)]
        scores = []
        for r0, hh in units:
            cs = slice(hh * HEAD_DIM, (hh + 1) * HEAD_DIM)
            s = _dot_nt(q_src[r0:r0 + WIN_BLOCK, cs], ks_ref[r0:r0 + 2 * WIN_BLOCK, cs])
            mask = band_first if r0 % sub_len == 0 else band
            scores.append(jnp.where(mask, s, -jnp.inf))
        s_all = jnp.concatenate(scores, axis=0)
        m = jnp.max(s_all, axis=-1, keepdims=True)
        p = jnp.exp(s_all - m)
        den = jnp.sum(p, axis=-1, keepdims=True)
        p = p.astype(BF16)
        inv = 1.0 / den
        lse_parts = _split3(m + jnp.log(den))
        for ui, (r0, hh) in enumerate(units):
            cs = slice(hh * HEAD_DIM, (hh + 1) * HEAD_DIM)
            us = slice(ui * WIN_BLOCK, (ui + 1) * WIN_BLOCK)
            pv = _dot(p[us], vs_ref[r0:r0 + 2 * WIN_BLOCK, cs])
            o_dst[r0:r0 + WIN_BLOCK, cs] = (pv * inv[us]).astype(BF16)
            if hh == 0:
                lse_tile = jnp.zeros((WIN_BLOCK, HEAD_DIM), F32)
            for part, lsplit in enumerate(lse_parts):
                lse_tile = jnp.where(lane == part * HEADS_PER_GROUP + hh, lsplit[us], lse_tile)
            if hh == HEADS_PER_GROUP - 1:
                l_dst[r0:r0 + WIN_BLOCK, :] = lse_tile.astype(BF16)

    if dil > 1:
        perm_t = _residue_permutation(dil, transpose=True)
        for nb in range(seq // PERM_BLOCK):
            rows = slice(nb * PERM_BLOCK, (nb + 1) * PERM_BLOCK)
            pieces = [slice(r * sub_len + nb * n, r * sub_len + (nb + 1) * n) for r in range(dil)]
            o_blk = jnp.concatenate([os_ref[pc, :] for pc in pieces], axis=0)
            o_ref[rows, :] = _dot(perm_t, o_blk).astype(BF16)
            l_blk = jnp.concatenate([ls_ref[pc, :] for pc in pieces], axis=0)
            lse_ref[rows, :] = _dot(perm_t, l_blk).astype(BF16)


def _dilated_attention(qk, vb, *, bsz, seq, group, rows_per_batch=512):
    tokens = qk.shape[0]
    dil = DIL_PATTERNS[group][1]
    kern = functools.partial(_dilated_kernel, dil=dil, rows_per_batch=rows_per_batch)
    scratch = [
        pltpu.VMEM((seq, GROUP_W), BF16),
        pltpu.VMEM((seq + WIN_BLOCK, GROUP_W), BF16),
        pltpu.VMEM((seq + WIN_BLOCK, GROUP_W), BF16),
        pltpu.VMEM((seq, GROUP_W), BF16),
        pltpu.VMEM((seq, HEAD_DIM), BF16),
    ]
    return pl.pallas_call(
        kern,
        grid=(bsz,),
        in_specs=[
            pl.BlockSpec((seq, GROUP_W), lambda b: (b, group)),
            pl.BlockSpec((seq, GROUP_W), lambda b: (b, N_GROUPS + group)),
            pl.BlockSpec((seq, GROUP_W), lambda b: (b, group)),
        ],
        out_specs=[
            pl.BlockSpec((seq, GROUP_W), lambda b: (b, 0)),
            pl.BlockSpec((seq, HEAD_DIM), lambda b: (b, 0)),
        ],
        out_shape=[
            jax.ShapeDtypeStruct((tokens, GROUP_W), BF16),
            jax.ShapeDtypeStruct((tokens, HEAD_DIM), BF16),
        ],
        scratch_shapes=scratch,
        compiler_params=_params("parallel"),
        name=f"dilated_attention_g{group}",
    )(qk, qk, vb)


def _softplus(z):
    return jnp.maximum(z, 0.0) + jnp.log(1.0 + jnp.exp(-jnp.abs(z)))


def _sb_kernel(q_ref, k_ref, v_ref, o_ref, *, tq, tk):
    qi = pl.program_id(1)
    heads = q_ref.shape[1] // HEAD_DIM
    diag_blocks = tq // tk
    m = heads * tq
    head_cols = [slice(hh * HEAD_DIM, (hh + 1) * HEAD_DIM) for hh in range(heads)]
    head_rows = [slice(hh * tq, (hh + 1) * tq) for hh in range(heads)]
    r = lax.broadcasted_iota(jnp.int32, (tk, tk), 0)
    c = lax.broadcasted_iota(jnp.int32, (tk, tk), 1)
    suffix = (r > c).astype(BF16)
    q_row = lax.broadcasted_iota(jnp.int32, (m, tk), 0) % tq
    k_col = lax.broadcasted_iota(jnp.int32, (m, tk), 1)

    def visit(kblk, acc, later, diag_offset):
        rows = pl.ds(pl.multiple_of(kblk * tk, tk), tk)
        z = jnp.concatenate([_dot_nt(q_ref[:, cs], k_ref[rows, cs]) for cs in head_cols],
                            axis=0)
        sp = _softplus(z)
        log_beta = z - sp
        log_1mb = -sp
        if diag_offset is not None:
            causal = k_col + diag_offset < q_row
            log_1mb = jnp.where(causal, log_1mb, 0.0)
        hi = log_1mb.astype(BF16)
        lo = (log_1mb - hi.astype(F32)).astype(BF16)
        after = _dot(hi, suffix) + _dot(lo, suffix) + later
        a = jnp.exp(log_beta + after)
        if diag_offset is not None:
            a = jnp.where(causal, a, 0.0)
        a = a.astype(BF16)
        pv = jnp.concatenate([_dot(a[rs], v_ref[rows, cs])
                              for rs, cs in zip(head_rows, head_cols)], axis=0)
        return acc + pv, later + jnp.sum(log_1mb, axis=-1, keepdims=True)

    acc = jnp.zeros((m, HEAD_DIM), F32)
    later = jnp.zeros((m, 1), F32)
    for d in reversed(range(diag_blocks)):
        acc, later = visit(qi * diag_blocks + d, acc, later, d * tk)
    acc, later = lax.fori_loop(
        0, qi * diag_blocks,
        lambda t, carry: visit(qi * diag_blocks - 1 - t, carry[0], carry[1], None),
        (acc, later))
    for rs, cs in zip(head_rows, head_cols):
        o_ref[:, cs] = acc[rs].astype(BF16)


def _stick_breaking(vb, *, bsz, seq, tq=512, tk=256):
    tokens = vb.shape[0]
    nq = seq // tq
    kern = functools.partial(_sb_kernel, tq=tq, tk=tk)
    return pl.pallas_call(
        kern,
        grid=(bsz, nq),
        in_specs=[
            pl.BlockSpec((tq, GROUP_W), lambda b, i: (b * nq + i, N_GROUPS)),
            pl.BlockSpec((seq, GROUP_W), lambda b, i: (b, N_GROUPS + 1)),
            pl.BlockSpec((seq, GROUP_W), lambda b, i: (b, N_GROUPS + 2)),
        ],
        out_specs=pl.BlockSpec((tq, GROUP_W), lambda b, i: (b * nq + i, 0)),
        out_shape=jax.ShapeDtypeStruct((tokens, GROUP_W), BF16),
        compiler_params=_params("parallel", "arbitrary"),
        name="stick_breaking",
    )(vb, vb, vb)


def _merge_kernel(h_ref, o0_ref, o1_ref, o2_ref, l0_ref, l1_ref, l2_ref, ob_ref,
                  sga_ref, sgb_ref, gate_ref, g2_ref, scale2_ref, shift2_ref,
                  wa_ref, wb_ref, wo_ref, out_ref, u2_ref, oa_ref):
    lses = []
    for l_ref in (l0_ref, l1_ref, l2_ref):
        lsplit = l_ref[...].astype(F32)
        lses.append([lsplit[:, hh:hh + 1] + lsplit[:, 4 + hh:5 + hh] + lsplit[:, 8 + hh:9 + hh]
                     for hh in range(HEADS_PER_GROUP)])
    for hh in range(HEADS_PER_GROUP):
        cs = slice(hh * HEAD_DIM, (hh + 1) * HEAD_DIM)
        l0, l1, l2 = lses[0][hh], lses[1][hh], lses[2][hh]
        m = jnp.maximum(jnp.maximum(l0, l1), l2)
        e0 = jnp.exp(l0 - m)
        e1 = jnp.exp(l1 - m)
        e2 = jnp.exp(l2 - m)
        inv = 1.0 / (e0 + e1 + e2)
        oa = ((e0 * inv) * o0_ref[:, cs].astype(F32)
              + (e1 * inv) * o1_ref[:, cs].astype(F32)
              + (e2 * inv) * o2_ref[:, cs].astype(F32))
        oa_ref[:, cs] = oa.astype(BF16)
    y_a = _dot(oa_ref[...], wa_ref[...])
    y_b = _dot(ob_ref[...], wb_ref[...])
    merged = sga_ref[...].astype(F32) * y_a + sgb_ref[...].astype(F32) * y_b
    h_new = h_ref[...] + gate_ref[...] * _dot(merged.astype(BF16), wo_ref[...])
    out_ref[...] = h_new
    u2_ref[...] = _modulated_norm(h_new, g2_ref[...], scale2_ref[...],
                                  shift2_ref[...]).astype(BF16)


def _merge(h, o_groups, lse_groups, o_b, gates, mod3, gate_idx, g2, shift2_idx, scale2_idx,
           wa, wb, wo, layer, *, seq, tm=256):
    tokens, d_model = h.shape
    blocks_per_seq = seq // tm
    row = lambda i: (i, 0)
    resident = lambda w: pl.BlockSpec((None,) + w.shape[1:], lambda i: (layer, 0, 0),
                                      pipeline_mode=pl.Buffered(1))
    mod_spec = lambda idx: pl.BlockSpec((None, 1, d_model),
                                        lambda i: (i // blocks_per_seq, 0, idx))
    return pl.pallas_call(
        _merge_kernel,
        grid=(tokens // tm,),
        in_specs=[
            pl.BlockSpec((tm, d_model), row),
            pl.BlockSpec((tm, GROUP_W), row),
            pl.BlockSpec((tm, GROUP_W), row),
            pl.BlockSpec((tm, GROUP_W), row),
            pl.BlockSpec((tm, HEAD_DIM), row),
            pl.BlockSpec((tm, HEAD_DIM), row),
            pl.BlockSpec((tm, HEAD_DIM), row),
            pl.BlockSpec((tm, GROUP_W), row),
            pl.BlockSpec((tm, d_model), lambda i: (i, 0)),
            pl.BlockSpec((tm, d_model), lambda i: (i, 1)),
            mod_spec(gate_idx),
            pl.BlockSpec((1, d_model), lambda i: (0, 0)),
            mod_spec(scale2_idx),
            mod_spec(shift2_idx),
            resident(wa),
            resident(wb),
            resident(wo),
        ],
        out_specs=[pl.BlockSpec((tm, d_model), row), pl.BlockSpec((tm, d_model), row)],
        out_shape=[jax.ShapeDtypeStruct((tokens, d_model), F32),
                   jax.ShapeDtypeStruct((tokens, d_model), BF16)],
        scratch_shapes=[pltpu.VMEM((tm, GROUP_W), BF16)],
        compiler_params=_params("parallel"),
        name="merge",
    )(h, *o_groups, *lse_groups, o_b, gates, gates, mod3, g2, mod3, mod3, wa, wb, wo)


def _ffn_kernel(u_ref, hres_ref, gate_ref, wg_ref, wu_ref, wd_ref, out_ref, act_ref, *, nf, tf):
    s = pl.program_id(1)

    @pl.when(s < nf)
    def _():
        u = u_ref[...]
        gate_act = _dot(u, wg_ref[...])
        up = _dot(u, wu_ref[...])
        cols = pl.ds(pl.multiple_of(s * tf, tf), tf)
        act_ref[:, cols] = (gate_act * jax.nn.sigmoid(gate_act) * up).astype(BF16)

    @pl.when(s >= nf)
    def _():
        out_ref[...] = hres_ref[...] + gate_ref[...] * _dot(act_ref[...], wd_ref[...])


def _ffn(h, u2, mod3, gate_idx, w_gate_up, w_down, layer, *, seq, tm=1024, tf=512, tn=256):
    tokens, d_model = h.shape
    d_ff = w_down.shape[1]
    nf = d_ff // tf
    nd = d_model // tn
    blocks_per_seq = seq // tm
    up_step = lambda s: jnp.minimum(s, nf - 1)
    down_step = lambda s: jnp.clip(s - nf, 0, nd - 1)
    kern = functools.partial(_ffn_kernel, nf=nf, tf=tf)
    return pl.pallas_call(
        kern,
        grid=(tokens // tm, nf + nd),
        in_specs=[
            pl.BlockSpec((tm, d_model), lambda i, s: (i, 0)),
            pl.BlockSpec((tm, tn), lambda i, s: (i, down_step(s))),
            pl.BlockSpec((None, 1, tn),
                         lambda i, s: (i // blocks_per_seq, 0, gate_idx * nd + down_step(s))),
            pl.BlockSpec((None, d_model, tf), lambda i, s: (layer, 0, up_step(s))),
            pl.BlockSpec((None, d_model, tf), lambda i, s: (layer, 0, nf + up_step(s))),
            pl.BlockSpec((None, d_ff, tn), lambda i, s: (layer, 0, down_step(s))),
        ],
        out_specs=pl.BlockSpec((tm, tn), lambda i, s: (i, down_step(s))),
        out_shape=jax.ShapeDtypeStruct((tokens, d_model), F32),
        scratch_shapes=[pltpu.VMEM((tm, d_ff), BF16)],
        compiler_params=_params("parallel", "arbitrary"),
        name="ffn",
    )(u2, h, mod3, w_gate_up, w_gate_up, w_down)


def _rope_tables(seq):
    inv = jnp.power(ROPE_THETA, -jnp.arange(0, HEAD_DIM, 2, dtype=F32) / HEAD_DIM)
    ang = jnp.arange(seq, dtype=F32)[:, None] * inv[None, :]
    cos, sin = jnp.cos(ang), jnp.sin(ang)
    return jnp.concatenate([cos, cos], axis=-1), jnp.concatenate([-sin, sin], axis=-1)


def kernel(x, c, w_ada, b_ada, norm1_g, norm2_g, w_in, qn_g, kn_g, w_branch_a, w_branch_b,
           w_out, w_gate_up, w_down):
    bsz, seq, d_model = x.shape
    depth = w_ada.shape[0]
    tokens = bsz * seq
    cos_full, sin_full = _rope_tables(seq)
    mod = _modulation(c, w_ada, b_ada)
    w_in, w_branch_a, w_branch_b, w_out, w_gate_up, w_down = (
        w.astype(BF16) for w in (w_in, w_branch_a, w_branch_b, w_out, w_gate_up, w_down))
    h = x.reshape(tokens, d_model)
    for l in range(depth):
        mod3 = mod[l].reshape(bsz, 1, 6 * d_model)
        u = _norm_modulate(h, norm1_g[l].reshape(1, d_model), mod3, 0, 1, seq)
        qk, vb, gates = _input_projections(u, w_in, l, cos_full, sin_full, qn_g[l], kn_g[l],
                                           seq=seq)
        outs = [_dilated_attention(qk, vb, bsz=bsz, seq=seq, group=g) for g in range(N_GROUPS)]
        o_b = _stick_breaking(vb, bsz=bsz, seq=seq)
        h, u2 = _merge(h, [o for o, _ in outs], [s for _, s in outs], o_b, gates, mod3, 2,
                       norm2_g[l].reshape(1, d_model), 3, 4,
                       w_branch_a, w_branch_b, w_out, l, seq=seq)
        h = _ffn(h, u2, mod3, 5, w_gate_up, w_down, l, seq=seq)
    return h.reshape(bsz, seq, d_model)
```

```python
import functools

import jax
import jax.numpy as jnp
from jax import lax
from jax.experimental import pallas as pl
from jax.experimental.pallas import tpu as pltpu

F32 = jnp.float32
BF16 = jnp.bfloat16

HEAD_DIM = 128
HEADS_PER_GROUP = 4
GROUP_W = HEADS_PER_GROUP * HEAD_DIM
DIL_PATTERNS = ((128, 1), (512, 4), (2048, 16))
N_GROUPS = len(DIL_PATTERNS)
WIN_BLOCK = 128
PERM_BLOCK = 256
ROPE_THETA = 10000.0
EPS = 1e-6
QK_SCALE = HEAD_DIM ** -0.5

VMEM_LIMIT_BYTES = 56 * 1024 * 1024


def _params(*semantics):
    return pltpu.CompilerParams(dimension_semantics=semantics,
                                vmem_limit_bytes=VMEM_LIMIT_BYTES)


def _dot(a, b):
    return jnp.dot(a, b, preferred_element_type=F32)


def _dot_nt(a, b):
    return lax.dot_general(a, b, (((1,), (1,)), ((), ())), preferred_element_type=F32)


def _mod_kernel(c_ref, w_ref, b_ref, o_ref):
    c = c_ref[...]
    c_act = (c * jax.nn.sigmoid(c)).astype(BF16)
    o_ref[...] = _dot(c_act, w_ref[...].astype(BF16)) + b_ref[...]


def _modulation(c, w_ada, b_ada, tn=512):
    depth, d_model, n = w_ada.shape
    bsz = c.shape[0]
    return pl.pallas_call(
        _mod_kernel,
        grid=(depth, n // tn),
        in_specs=[
            pl.BlockSpec((bsz, d_model), lambda l, j: (0, 0)),
            pl.BlockSpec((None, d_model, tn), lambda l, j: (l, 0, j)),
            pl.BlockSpec((None, 1, tn), lambda l, j: (l, 0, j)),
        ],
        out_specs=pl.BlockSpec((None, bsz, tn), lambda l, j: (l, 0, j)),
        out_shape=jax.ShapeDtypeStruct((depth, bsz, n), F32),
        compiler_params=_params("parallel", "parallel"),
        name="modulation",
    )(c, w_ada, b_ada.reshape(depth, 1, n))


def _modulated_norm(h, g, scale, shift):
    ms = jnp.mean(h * h, axis=-1, keepdims=True)
    return (h * lax.rsqrt(ms + EPS)) * g * (1.0 + scale) + shift


def _norm_kernel(h_ref, g_ref, scale_ref, shift_ref, o_ref):
    o_ref[...] = _modulated_norm(h_ref[...], g_ref[...], scale_ref[...],
                                 shift_ref[...]).astype(BF16)


def _norm_modulate(h, g, mod3, shift_idx, scale_idx, seq, tm=512):
    tokens, d_model = h.shape
    blocks_per_seq = seq // tm
    return pl.pallas_call(
        _norm_kernel,
        grid=(tokens // tm,),
        in_specs=[
            pl.BlockSpec((tm, d_model), lambda i: (i, 0)),
            pl.BlockSpec((1, d_model), lambda i: (0, 0)),
            pl.BlockSpec((None, 1, d_model), lambda i: (i // blocks_per_seq, 0, scale_idx)),
            pl.BlockSpec((None, 1, d_model), lambda i: (i // blocks_per_seq, 0, shift_idx)),
        ],
        out_specs=pl.BlockSpec((tm, d_model), lambda i: (i, 0)),
        out_shape=jax.ShapeDtypeStruct((tokens, d_model), BF16),
        compiler_params=_params("parallel"),
        name="norm_modulate",
    )(h, g, mod3, mod3)


def _rope_epilogue(acc, cos_ref, sin_ref, gain_ref, o_ref):
    cos = cos_ref[...]
    sin = sin_ref[...]
    gain = gain_ref[...]
    for hh in range(HEADS_PER_GROUP):
        cs = slice(hh * HEAD_DIM, (hh + 1) * HEAD_DIM)
        x = acc[:, cs]
        ms = jnp.mean(x * x, axis=-1, keepdims=True)
        y = (x * lax.rsqrt(ms + EPS)) * gain
        o_ref[:, cs] = (y * cos + pltpu.roll(y, HEAD_DIM // 2, 1) * sin).astype(BF16)


def _scaled_epilogue(acc, mult_ref, o_ref):
    o_ref[...] = (acc * mult_ref[...]).astype(BF16)


def _sigmoid_epilogue(acc, o_ref):
    o_ref[...] = jax.nn.sigmoid(acc).astype(BF16)


def _proj_kernel(u_ref, w_ref, *refs, epilogue):
    *extra_refs, o_ref, acc_ref = refs

    @pl.when(pl.program_id(0) == 0)
    def _():
        acc_ref[...] = jnp.zeros(acc_ref.shape, F32)

    epilogue(acc_ref[...], *extra_refs, o_ref)
    acc_ref[...] = _dot(u_ref[...], w_ref[...])


def _projection(epilogue, name, u, w, layer, first_tile, n_tiles, extra_specs, extra_args,
                tm=1024):
    tokens, d_model = u.shape
    n_pairs = (tokens // tm) * n_tiles
    mm = lambda t: jnp.minimum(t, n_pairs - 1)
    ep = lambda t: jnp.maximum(t - 1, 0)
    lagged = lambda spec: pl.BlockSpec(
        spec.block_shape, lambda t, f=spec.index_map: f(ep(t) // n_tiles, ep(t) % n_tiles))
    return pl.pallas_call(
        functools.partial(_proj_kernel, epilogue=epilogue),
        grid=(n_pairs + 1,),
        in_specs=[
            pl.BlockSpec((tm, d_model), lambda t: (mm(t) // n_tiles, 0)),
            pl.BlockSpec((None, d_model, GROUP_W),
                         lambda t: (layer, 0, first_tile + mm(t) % n_tiles)),
        ] + [lagged(spec) for spec in extra_specs],
        out_specs=lagged(pl.BlockSpec((tm, GROUP_W), lambda i, j: (i, j))),
        out_shape=jax.ShapeDtypeStruct((tokens, n_tiles * GROUP_W), BF16),
        scratch_shapes=[pltpu.VMEM((tm, GROUP_W), F32)],
        compiler_params=_params("arbitrary"),
        name=name,
    )(u, w, *extra_args)


def _input_projections(u, w_in, layer, cos_full, sin_full, qn_g, kn_g, *, seq, tm=1024):
    d_model = u.shape[1]
    blocks_per_seq = seq // tm
    rope_spec = pl.BlockSpec((tm, HEAD_DIM), lambda i, j: (i % blocks_per_seq, 0))
    gains = jnp.stack([qn_g * QK_SCALE, kn_g]).reshape(2, 1, HEAD_DIM)
    qk = _projection(
        _rope_epilogue, "projection_qk", u, w_in, layer, 0, 2 * N_GROUPS,
        [rope_spec, rope_spec,
         pl.BlockSpec((None, 1, HEAD_DIM), lambda i, j: (j // N_GROUPS, 0, 0))],
        [cos_full, sin_full, gains], tm=tm)
    n_vb = N_GROUPS + 3
    mult = jnp.ones((n_vb, 1, GROUP_W), F32).at[N_GROUPS].set(QK_SCALE)
    vb = _projection(
        _scaled_epilogue, "projection_vb", u, w_in, layer, 2 * N_GROUPS, n_vb,
        [pl.BlockSpec((None, 1, GROUP_W), lambda i, j: (j, 0, 0))], [mult], tm=tm)
    gates = _projection(
        _sigmoid_epilogue, "projection_gates", u, w_in, layer, 2 * N_GROUPS + n_vb,
        2 * d_model // GROUP_W, [], [], tm=tm)
    return qk, vb, gates


def _residue_permutation(dil, transpose):
    n = PERM_BLOCK // dil
    a = lax.broadcasted_iota(jnp.int32, (PERM_BLOCK, PERM_BLOCK), 1 if transpose else 0)
    b = lax.broadcasted_iota(jnp.int32, (PERM_BLOCK, PERM_BLOCK), 0 if transpose else 1)
    return (b == (a % n) * dil + a // n).astype(BF16)


def _split3(x):
    h1 = x.astype(BF16).astype(F32)
    r1 = x - h1
    h2 = r1.astype(BF16).astype(F32)
    h3 = (r1 - h2).astype(BF16).astype(F32)
    return h1, h2, h3


def _dilated_kernel(q_ref, k_ref, v_ref, o_ref, lse_ref, qs_ref, ks_ref, vs_ref, os_ref, ls_ref,
                    *, dil, rows_per_batch):
    seq = q_ref.shape[0]
    sub_len = seq // dil
    n = PERM_BLOCK // dil
    pad = WIN_BLOCK

    zeros = jnp.zeros((pad, GROUP_W), BF16)
    ks_ref[0:pad, :] = zeros
    vs_ref[0:pad, :] = zeros
    if dil == 1:
        ks_ref[pad:, :] = k_ref[...]
        vs_ref[pad:, :] = v_ref[...]
        q_src = q_ref
    else:
        perm = _residue_permutation(dil, transpose=False)
        for src, dst, off in ((q_ref, qs_ref, 0), (k_ref, ks_ref, pad), (v_ref, vs_ref, pad)):
            for nb in range(seq // PERM_BLOCK):
                y = _dot(perm, src[nb * PERM_BLOCK:(nb + 1) * PERM_BLOCK, :]).astype(BF16)
                for r in range(dil):
                    d0 = off + r * sub_len + nb * n
                    dst[d0:d0 + n, :] = y[r * n:(r + 1) * n, :]
        q_src = qs_ref
    o_dst, l_dst = (o_ref, lse_ref) if dil == 1 else (os_ref, ls_ref)

    qi_ = lax.broadcasted_iota(jnp.int32, (WIN_BLOCK, 2 * WIN_BLOCK), 0)
    kj_ = lax.broadcasted_iota(jnp.int32, (WIN_BLOCK, 2 * WIN_BLOCK), 1)
    band = (kj_ >= qi_) & (kj_ <= qi_ + WIN_BLOCK)
    band_first = band & (kj_ >= WIN_BLOCK)
    lane = lax.broadcasted_iota(jnp.int32, (WIN_BLOCK, HEAD_DIM), 1)
    n_qb = rows_per_batch // WIN_BLOCK
    for b0 in range(0, seq, rows_per_batch):
        units = [(b0 + qb * WIN_BLOCK, hh) for qb in range(n_qb) for hh in range(HEADS_PER_GROUP)]
        scores = []
        for r0, hh in units:
            cs = slice(hh * HEAD_DIM, (hh + 1) * HEAD_DIM)
            s = _dot_nt(q_src[r0:r0 + WIN_BLOCK, cs], ks_ref[r0:r0 + 2 * WIN_BLOCK, cs])
            mask = band_first if r0 % sub_len == 0 else band
            scores.append(jnp.where(mask, s, -jnp.inf))
        s_all = jnp.concatenate(scores, axis=0)
        m = jnp.max(s_all, axis=-1, keepdims=True)
        p = jnp.exp(s_all - m)
        den = jnp.sum(p, axis=-1, keepdims=True)
        p = p.astype(BF16)
        inv = 1.0 / den
        lse_parts = _split3(m + jnp.log(den))
        for ui, (r0, hh) in enumerate(units):
            cs = slice(hh * HEAD_DIM, (hh + 1) * HEAD_DIM)
            us = slice(ui * WIN_BLOCK, (ui + 1) * WIN_BLOCK)
            pv = _dot(p[us], vs_ref[r0:r0 + 2 * WIN_BLOCK, cs])
            o_dst[r0:r0 + WIN_BLOCK, cs] = (pv * inv[us]).astype(BF16)
            if hh == 0:
                lse_tile = jnp.zeros((WIN_BLOCK, HEAD_DIM), F32)
            for part, lsplit in enumerate(lse_parts):
                lse_tile = jnp.where(lane == part * HEADS_PER_GROUP + hh, lsplit[us], lse_tile)
            if hh == HEADS_PER_GROUP - 1:
                l_dst[r0:r0 + WIN_BLOCK, :] = lse_tile.astype(BF16)

    if dil > 1:
        perm_t = _residue_permutation(dil, transpose=True)
        for nb in range(seq // PERM_BLOCK):
            rows = slice(nb * PERM_BLOCK, (nb + 1) * PERM_BLOCK)
            pieces = [slice(r * sub_len + nb * n, r * sub_len + (nb + 1) * n) for r in range(dil)]
            o_blk = jnp.concatenate([os_ref[pc, :] for pc in pieces], axis=0)
            o_ref[rows, :] = _dot(perm_t, o_blk).astype(BF16)
            l_blk = jnp.concatenate([ls_ref[pc, :] for pc in pieces], axis=0)
            lse_ref[rows, :] = _dot(perm_t, l_blk).astype(BF16)


def _dilated_attention(qk, vb, *, bsz, seq, group, rows_per_batch=512):
    tokens = qk.shape[0]
    dil = DIL_PATTERNS[group][1]
    kern = functools.partial(_dilated_kernel, dil=dil, rows_per_batch=rows_per_batch)
    scratch = [
        pltpu.VMEM((seq, GROUP_W), BF16),
        pltpu.VMEM((seq + WIN_BLOCK, GROUP_W), BF16),
        pltpu.VMEM((seq + WIN_BLOCK, GROUP_W), BF16),
        pltpu.VMEM((seq, GROUP_W), BF16),
        pltpu.VMEM((seq, HEAD_DIM), BF16),
    ]
    return pl.pallas_call(
        kern,
        grid=(bsz,),
        in_specs=[
            pl.BlockSpec((seq, GROUP_W), lambda b: (b, group)),
            pl.BlockSpec((seq, GROUP_W), lambda b: (b, N_GROUPS + group)),
            pl.BlockSpec((seq, GROUP_W), lambda b: (b, group)),
        ],
        out_specs=[
            pl.BlockSpec((seq, GROUP_W), lambda b: (b, 0)),
            pl.BlockSpec((seq, HEAD_DIM), lambda b: (b, 0)),
        ],
        out_shape=[
            jax.ShapeDtypeStruct((tokens, GROUP_W), BF16),
            jax.ShapeDtypeStruct((tokens, HEAD_DIM), BF16),
        ],
        scratch_shapes=scratch,
        compiler_params=_params("parallel"),
        name=f"dilated_attention_g{group}",
    )(qk, qk, vb)


def _sb_kernel(q_ref, k_ref, v_ref, o_ref, *, tq, tk):
    qi = pl.program_id(1)
    heads = q_ref.shape[1] // HEAD_DIM
    diag_blocks = tq // tk
    m = heads * tq
    head_cols = [slice(hh * HEAD_DIM, (hh + 1) * HEAD_DIM) for hh in range(heads)]
    head_rows = [slice(hh * tq, (hh + 1) * tq) for hh in range(heads)]
    r = lax.broadcasted_iota(jnp.int32, (tk, tk), 0)
    c = lax.broadcasted_iota(jnp.int32, (tk, tk), 1)
    suffix = (r > c).astype(BF16)
    q_row = lax.broadcasted_iota(jnp.int32, (m, tk), 0) % tq
    k_col = lax.broadcasted_iota(jnp.int32, (m, tk), 1)

    def visit(kblk, acc, later, diag_offset):
        rows = pl.ds(pl.multiple_of(kblk * tk, tk), tk)
        z = jnp.concatenate([_dot_nt(q_ref[:, cs], k_ref[rows, cs]) for cs in head_cols],
                            axis=0)
        neg_abs = pltpu.bitcast(pltpu.bitcast(z, jnp.uint32) | jnp.uint32(0x80000000), F32)
        sp = jnp.log(1.0 + jnp.exp(neg_abs)) + jnp.maximum(z, 0.0)
        log_beta = z - sp
        if diag_offset is not None:
            causal = k_col + diag_offset < q_row
            sp = jnp.where(causal, sp, 0.0)
        local = _dot(sp.astype(BF16), suffix)
        a = jnp.exp(log_beta - local)
        if diag_offset is not None:
            a = jnp.where(causal, a, 0.0)
        a = a.astype(BF16)
        pv = jnp.concatenate([_dot(a[rs], v_ref[rows, cs])
                              for rs, cs in zip(head_rows, head_cols)], axis=0)
        return (acc + jnp.exp(-later) * pv,
                later + (local[:, 0:1] + sp[:, 0:1]))

    acc = jnp.zeros((m, HEAD_DIM), F32)
    later = jnp.zeros((m, 1), F32)
    for d in reversed(range(diag_blocks)):
        acc, later = visit(qi * diag_blocks + d, acc, later, d * tk)
    acc, later = lax.fori_loop(
        0, qi * diag_blocks,
        lambda t, carry: visit(qi * diag_blocks - 1 - t, carry[0], carry[1], None),
        (acc, later))
    for rs, cs in zip(head_rows, head_cols):
        o_ref[:, cs] = acc[rs].astype(BF16)


def _stick_breaking(vb, *, bsz, seq, tq=512, tk=256):
    tokens = vb.shape[0]
    nq = seq // tq
    kern = functools.partial(_sb_kernel, tq=tq, tk=tk)
    return pl.pallas_call(
        kern,
        grid=(bsz, nq),
        in_specs=[
            pl.BlockSpec((tq, GROUP_W), lambda b, i: (b * nq + i, N_GROUPS)),
            pl.BlockSpec((seq, GROUP_W), lambda b, i: (b, N_GROUPS + 1)),
            pl.BlockSpec((seq, GROUP_W), lambda b, i: (b, N_GROUPS + 2)),
        ],
        out_specs=pl.BlockSpec((tq, GROUP_W), lambda b, i: (b * nq + i, 0)),
        out_shape=jax.ShapeDtypeStruct((tokens, GROUP_W), BF16),
        compiler_params=_params("parallel", "arbitrary"),
        name="stick_breaking",
    )(vb, vb, vb)


def _merge_kernel(h_ref, o0_ref, o1_ref, o2_ref, l0_ref, l1_ref, l2_ref, ob_ref,
                  sga_ref, sgb_ref, gate_ref, g2_ref, scale2_ref, shift2_ref,
                  wa_ref, wb_ref, wo_ref, out_ref, u2_ref, oa_ref):
    lses = []
    for l_ref in (l0_ref, l1_ref, l2_ref):
        lsplit = l_ref[...].astype(F32)
        lses.append([lsplit[:, hh:hh + 1] + lsplit[:, 4 + hh:5 + hh] + lsplit[:, 8 + hh:9 + hh]
                     for hh in range(HEADS_PER_GROUP)])
    for hh in range(HEADS_PER_GROUP):
        cs = slice(hh * HEAD_DIM, (hh + 1) * HEAD_DIM)
        l0, l1, l2 = lses[0][hh], lses[1][hh], lses[2][hh]
        m = jnp.maximum(jnp.maximum(l0, l1), l2)
        e0 = jnp.exp(l0 - m)
        e1 = jnp.exp(l1 - m)
        e2 = jnp.exp(l2 - m)
        inv = 1.0 / (e0 + e1 + e2)
        oa = ((e0 * inv) * o0_ref[:, cs].astype(F32)
              + (e1 * inv) * o1_ref[:, cs].astype(F32)
              + (e2 * inv) * o2_ref[:, cs].astype(F32))
        oa_ref[:, cs] = oa.astype(BF16)
    y_a = _dot(oa_ref[...], wa_ref[...])
    y_b = _dot(ob_ref[...], wb_ref[...])
    merged = sga_ref[...].astype(F32) * y_a + sgb_ref[...].astype(F32) * y_b
    h_new = h_ref[...] + gate_ref[...] * _dot(merged.astype(BF16), wo_ref[...])
    out_ref[...] = h_new
    u2_ref[...] = _modulated_norm(h_new, g2_ref[...], scale2_ref[...],
                                  shift2_ref[...]).astype(BF16)


def _merge(h, o_groups, lse_groups, o_b, gates, mod3, gate_idx, g2, shift2_idx, scale2_idx,
           wa, wb, wo, layer, *, seq, tm=512):
    tokens, d_model = h.shape
    blocks_per_seq = seq // tm
    row = lambda i: (i, 0)
    resident = lambda w: pl.BlockSpec((None,) + w.shape[1:], lambda i: (layer, 0, 0),
                                      pipeline_mode=pl.Buffered(1))
    mod_spec = lambda idx: pl.BlockSpec((None, 1, d_model),
                                        lambda i: (i // blocks_per_seq, 0, idx))
    return pl.pallas_call(
        _merge_kernel,
        grid=(tokens // tm,),
        in_specs=[
            pl.BlockSpec((tm, d_model), row),
            pl.BlockSpec((tm, GROUP_W), row),
            pl.BlockSpec((tm, GROUP_W), row),
            pl.BlockSpec((tm, GROUP_W), row),
            pl.BlockSpec((tm, HEAD_DIM), row),
            pl.BlockSpec((tm, HEAD_DIM), row),
            pl.BlockSpec((tm, HEAD_DIM), row),
            pl.BlockSpec((tm, GROUP_W), row),
            pl.BlockSpec((tm, d_model), lambda i: (i, 0)),
            pl.BlockSpec((tm, d_model), lambda i: (i, 1)),
            mod_spec(gate_idx),
            pl.BlockSpec((1, d_model), lambda i: (0, 0)),
            mod_spec(scale2_idx),
            mod_spec(shift2_idx),
            resident(wa),
            resident(wb),
            resident(wo),
        ],
        out_specs=[pl.BlockSpec((tm, d_model), row), pl.BlockSpec((tm, d_model), row)],
        out_shape=[jax.ShapeDtypeStruct((tokens, d_model), F32),
                   jax.ShapeDtypeStruct((tokens, d_model), BF16)],
        scratch_shapes=[pltpu.VMEM((tm, GROUP_W), BF16)],
        compiler_params=_params("parallel"),
        name="merge",
    )(h, *o_groups, *lse_groups, o_b, gates, gates, mod3, g2, mod3, mod3, wa, wb, wo)


def _ffn_kernel(u_ref, hres_ref, gate_ref, wg_ref, wu_ref, wd_ref, out_ref, act_ref, *, nf, tf):
    s = pl.program_id(1)

    @pl.when(s < nf)
    def _():
        u = u_ref[...]
        gate_act = _dot(u, wg_ref[...])
        up = _dot(u, wu_ref[...])
        cols = pl.ds(pl.multiple_of(s * tf, tf), tf)
        act_ref[:, cols] = (gate_act * jax.nn.sigmoid(gate_act) * up).astype(BF16)

    @pl.when(s >= nf)
    def _():
        out_ref[...] = hres_ref[...] + gate_ref[...] * _dot(act_ref[...], wd_ref[...])


def _ffn(h, u2, mod3, gate_idx, w_gate_up, w_down, layer, *, seq, tm=1024, tf=512, tn=256):
    tokens, d_model = h.shape
    d_ff = w_down.shape[1]
    nf = d_ff // tf
    nd = d_model // tn
    blocks_per_seq = seq // tm
    up_step = lambda s: jnp.minimum(s, nf - 1)
    down_step = lambda s: jnp.clip(s - nf, 0, nd - 1)
    kern = functools.partial(_ffn_kernel, nf=nf, tf=tf)
    return pl.pallas_call(
        kern,
        grid=(tokens // tm, nf + nd),
        in_specs=[
            pl.BlockSpec((tm, d_model), lambda i, s: (i, 0)),
            pl.BlockSpec((tm, tn), lambda i, s: (i, down_step(s))),
            pl.BlockSpec((None, 1, tn),
                         lambda i, s: (i // blocks_per_seq, 0, gate_idx * nd + down_step(s))),
            pl.BlockSpec((None, d_model, tf), lambda i, s: (layer, 0, up_step(s))),
            pl.BlockSpec((None, d_model, tf), lambda i, s: (layer, 0, nf + up_step(s))),
            pl.BlockSpec((None, d_ff, tn), lambda i, s: (layer, 0, down_step(s))),
        ],
        out_specs=pl.BlockSpec((tm, tn), lambda i, s: (i, down_step(s))),
        out_shape=jax.ShapeDtypeStruct((tokens, d_model), F32),
        scratch_shapes=[pltpu.VMEM((tm, d_ff), BF16)],
        compiler_params=_params("parallel", "arbitrary"),
        name="ffn",
    )(u2, h, mod3, w_gate_up, w_gate_up, w_down)


def _rope_tables(seq):
    inv = jnp.power(ROPE_THETA, -jnp.arange(0, HEAD_DIM, 2, dtype=F32) / HEAD_DIM)
    ang = jnp.arange(seq, dtype=F32)[:, None] * inv[None, :]
    cos, sin = jnp.cos(ang), jnp.sin(ang)
    return jnp.concatenate([cos, cos], axis=-1), jnp.concatenate([-sin, sin], axis=-1)


def kernel(x, c, w_ada, b_ada, norm1_g, norm2_g, w_in, qn_g, kn_g, w_branch_a, w_branch_b,
           w_out, w_gate_up, w_down):
    bsz, seq, d_model = x.shape
    depth = w_ada.shape[0]
    tokens = bsz * seq
    cos_full, sin_full = _rope_tables(seq)
    mod = _modulation(c, w_ada, b_ada)
    w_in, w_branch_a, w_branch_b, w_out, w_gate_up, w_down = (
        w.astype(BF16) for w in (w_in, w_branch_a, w_branch_b, w_out, w_gate_up, w_down))
    h = x.reshape(tokens, d_model)
    for l in range(depth):
        mod3 = mod[l].reshape(bsz, 1, 6 * d_model)
        u = _norm_modulate(h, norm1_g[l].reshape(1, d_model), mod3, 0, 1, seq)
        qk, vb, gates = _input_projections(u, w_in, l, cos_full, sin_full, qn_g[l], kn_g[l],
                                           seq=seq)
        outs = [_dilated_attention(qk, vb, bsz=bsz, seq=seq, group=g) for g in range(N_GROUPS)]
        o_b = _stick_breaking(vb, bsz=bsz, seq=seq)
        h, u2 = _merge(h, [o for o, _ in outs], [s for _, s in outs], o_b, gates, mod3, 2,
                       norm2_g[l].reshape(1, d_model), 3, 4,
                       w_branch_a, w_branch_b, w_out, l, seq=seq)
        h = _ffn(h, u2, mod3, 5, w_gate_up, w_down, l, seq=seq)
    return h.reshape(bsz, seq, d_model)
```

```python
import functools

import jax
import jax.numpy as jnp
from jax import lax
from jax.experimental import pallas as pl
from jax.experimental.pallas import tpu as pltpu

F32 = jnp.float32
BF16 = jnp.bfloat16

HEAD_DIM = 128
HEADS_PER_GROUP = 4
GROUP_W = HEADS_PER_GROUP * HEAD_DIM
DIL_PATTERNS = ((128, 1), (512, 4), (2048, 16))
N_GROUPS = len(DIL_PATTERNS)
WIN_BLOCK = 128
PERM_BLOCK = 256
ROPE_THETA = 10000.0
EPS = 1e-6
QK_SCALE = HEAD_DIM ** -0.5

VMEM_LIMIT_BYTES = 56 * 1024 * 1024


def _params(*semantics):
    return pltpu.CompilerParams(dimension_semantics=semantics,
                                vmem_limit_bytes=VMEM_LIMIT_BYTES)


def _dot(a, b):
    return jnp.dot(a, b, preferred_element_type=F32)


def _dot_nt(a, b):
    return lax.dot_general(a, b, (((1,), (1,)), ((), ())), preferred_element_type=F32)


def _mod_kernel(c_ref, w_ref, b_ref, o_ref):
    c = c_ref[...]
    c_act = (c * jax.nn.sigmoid(c)).astype(BF16)
    o_ref[...] = _dot(c_act, w_ref[...].astype(BF16)) + b_ref[...]


def _modulation(c, w_ada, b_ada, tn=512):
    depth, d_model, n = w_ada.shape
    bsz = c.shape[0]
    return pl.pallas_call(
        _mod_kernel,
        grid=(depth, n // tn),
        in_specs=[
            pl.BlockSpec((bsz, d_model), lambda l, j: (0, 0)),
            pl.BlockSpec((None, d_model, tn), lambda l, j: (l, 0, j)),
            pl.BlockSpec((None, 1, tn), lambda l, j: (l, 0, j)),
        ],
        out_specs=pl.BlockSpec((None, bsz, tn), lambda l, j: (l, 0, j)),
        out_shape=jax.ShapeDtypeStruct((depth, bsz, n), F32),
        compiler_params=_params("parallel", "parallel"),
        name="modulation",
    )(c, w_ada, b_ada.reshape(depth, 1, n))


def _modulated_norm(h, g, scale, shift):
    ms = jnp.mean(h * h, axis=-1, keepdims=True)
    return (h * lax.rsqrt(ms + EPS)) * g * (1.0 + scale) + shift


def _norm_kernel(h_ref, g_ref, scale_ref, shift_ref, o_ref):
    o_ref[...] = _modulated_norm(h_ref[...], g_ref[...], scale_ref[...],
                                 shift_ref[...]).astype(BF16)


def _norm_modulate(h, g, mod3, shift_idx, scale_idx, seq, tm=512):
    tokens, d_model = h.shape
    blocks_per_seq = seq // tm
    return pl.pallas_call(
        _norm_kernel,
        grid=(tokens // tm,),
        in_specs=[
            pl.BlockSpec((tm, d_model), lambda i: (i, 0)),
            pl.BlockSpec((1, d_model), lambda i: (0, 0)),
            pl.BlockSpec((None, 1, d_model), lambda i: (i // blocks_per_seq, 0, scale_idx)),
            pl.BlockSpec((None, 1, d_model), lambda i: (i // blocks_per_seq, 0, shift_idx)),
        ],
        out_specs=pl.BlockSpec((tm, d_model), lambda i: (i, 0)),
        out_shape=jax.ShapeDtypeStruct((tokens, d_model), BF16),
        compiler_params=_params("parallel"),
        name="norm_modulate",
    )(h, g, mod3, mod3)


def _rope_epilogue(acc, cos_ref, sin_ref, gain_ref, o_ref):
    cos = cos_ref[...]
    sin = sin_ref[...]
    for hh in range(acc.shape[1] // HEAD_DIM):
        cs = slice(hh * HEAD_DIM, (hh + 1) * HEAD_DIM)
        x = acc[:, cs]
        ms = jnp.mean(x * x, axis=-1, keepdims=True)
        y = (x * lax.rsqrt(ms + EPS)) * gain_ref[:, cs]
        o_ref[:, cs] = (y * cos + pltpu.roll(y, HEAD_DIM // 2, 1) * sin).astype(BF16)


def _scaled_epilogue(acc, mult_ref, o_ref):
    o_ref[...] = (acc * mult_ref[...]).astype(BF16)


def _sigmoid_epilogue(acc, o_ref):
    o_ref[...] = jax.nn.sigmoid(acc).astype(BF16)


def _proj_kernel(u_ref, w_ref, *refs, epilogue):
    *extra_refs, o_ref, acc_ref = refs

    @pl.when(pl.program_id(0) == 0)
    def _():
        acc_ref[...] = jnp.zeros(acc_ref.shape, F32)

    epilogue(acc_ref[...], *extra_refs, o_ref)
    acc_ref[...] = _dot(u_ref[...], w_ref[...])


def _projection(epilogue, name, u, w, layer, first_col, n_cols, tile_w, extra_specs, extra_args,
                tm=1024):
    tokens, d_model = u.shape
    n_tiles = n_cols // tile_w
    first_tile = first_col // tile_w
    n_pairs = (tokens // tm) * n_tiles
    mm = lambda t: jnp.minimum(t, n_pairs - 1)
    ep = lambda t: jnp.maximum(t - 1, 0)
    lagged = lambda spec: pl.BlockSpec(
        spec.block_shape, lambda t, f=spec.index_map: f(ep(t) // n_tiles, ep(t) % n_tiles))
    return pl.pallas_call(
        functools.partial(_proj_kernel, epilogue=epilogue),
        grid=(n_pairs + 1,),
        in_specs=[
            pl.BlockSpec((tm, d_model), lambda t: (mm(t) // n_tiles, 0)),
            pl.BlockSpec((None, d_model, tile_w),
                         lambda t: (layer, 0, first_tile + mm(t) % n_tiles)),
        ] + [lagged(spec) for spec in extra_specs],
        out_specs=lagged(pl.BlockSpec((tm, tile_w), lambda i, j: (i, j))),
        out_shape=jax.ShapeDtypeStruct((tokens, n_cols), BF16),
        scratch_shapes=[pltpu.VMEM((tm, tile_w), F32)],
        compiler_params=_params("arbitrary"),
        name=name,
    )(u, w, *extra_args)


def _input_projections(u, w_in, layer, cos_full, sin_full, qn_g, kn_g, *, seq, tm=1024):
    d_model = u.shape[1]
    blocks_per_seq = seq // tm
    a_w = N_GROUPS * GROUP_W
    col_spec = pl.BlockSpec((1, a_w), lambda i, j: (0, j))
    rope_spec = pl.BlockSpec((tm, HEAD_DIM), lambda i, j: (i % blocks_per_seq, 0))
    gains = jnp.concatenate([jnp.tile(qn_g * QK_SCALE, a_w // HEAD_DIM),
                             jnp.tile(kn_g, a_w // HEAD_DIM)]).reshape(1, 2 * a_w)
    qk = _projection(_rope_epilogue, "projection_qk", u, w_in, layer, 0, 2 * a_w, a_w,
                     [rope_spec, rope_spec, col_spec], [cos_full, sin_full, gains], tm=tm)
    vb_w = a_w + 3 * GROUP_W
    mult = jnp.ones((1, vb_w), F32).at[:, a_w:a_w + GROUP_W].set(QK_SCALE)
    vb = _projection(_scaled_epilogue, "projection_vb", u, w_in, layer, 2 * a_w, vb_w, a_w,
                     [col_spec], [mult], tm=tm)
    gates = _projection(_sigmoid_epilogue, "projection_gates", u, w_in, layer, 2 * a_w + vb_w,
                        2 * d_model, d_model, [], [], tm=tm)
    return qk, vb, gates


def _residue_permutation(dil, transpose):
    n = PERM_BLOCK // dil
    a = lax.broadcasted_iota(jnp.int32, (PERM_BLOCK, PERM_BLOCK), 1 if transpose else 0)
    b = lax.broadcasted_iota(jnp.int32, (PERM_BLOCK, PERM_BLOCK), 0 if transpose else 1)
    return (b == (a % n) * dil + a // n).astype(BF16)


def _split3(x):
    h1 = x.astype(BF16).astype(F32)
    r1 = x - h1
    h2 = r1.astype(BF16).astype(F32)
    h3 = (r1 - h2).astype(BF16).astype(F32)
    return h1, h2, h3


def _dilated_kernel(q_ref, k_ref, v_ref, o_ref, lse_ref, qs_ref, ks_ref, vs_ref, os_ref, ls_ref,
                    *, dil, rows_per_batch):
    seq = q_ref.shape[0]
    sub_len = seq // dil
    n = PERM_BLOCK // dil
    pad = WIN_BLOCK

    zeros = jnp.zeros((pad, GROUP_W), BF16)
    ks_ref[0:pad, :] = zeros
    vs_ref[0:pad, :] = zeros
    if dil == 1:
        ks_ref[pad:, :] = k_ref[...]
        vs_ref[pad:, :] = v_ref[...]
        q_src = q_ref
    else:
        perm = _residue_permutation(dil, transpose=False)
        for src, dst, off in ((q_ref, qs_ref, 0), (k_ref, ks_ref, pad), (v_ref, vs_ref, pad)):
            for nb in range(seq // PERM_BLOCK):
                y = _dot(perm, src[nb * PERM_BLOCK:(nb + 1) * PERM_BLOCK, :]).astype(BF16)
                for r in range(dil):
                    d0 = off + r * sub_len + nb * n
                    dst[d0:d0 + n, :] = y[r * n:(r + 1) * n, :]
        q_src = qs_ref
    o_dst, l_dst = (o_ref, lse_ref) if dil == 1 else (os_ref, ls_ref)

    qi_ = lax.broadcasted_iota(jnp.int32, (WIN_BLOCK, 2 * WIN_BLOCK), 0)
    kj_ = lax.broadcasted_iota(jnp.int32, (WIN_BLOCK, 2 * WIN_BLOCK), 1)
    band = (kj_ >= qi_) & (kj_ <= qi_ + WIN_BLOCK)
    band_first = band & (kj_ >= WIN_BLOCK)
    lane = lax.broadcasted_iota(jnp.int32, (WIN_BLOCK, HEAD_DIM), 1)
    n_qb = rows_per_batch // WIN_BLOCK
    for b0 in range(0, seq, rows_per_batch):
        units = [(b0 + qb * WIN_BLOCK, hh) for qb in range(n_qb) for hh in range(HEADS_PER_GROUP)]
        scores = []
        for r0, hh in units:
            cs = slice(hh * HEAD_DIM, (hh + 1) * HEAD_DIM)
            s = _dot_nt(q_src[r0:r0 + WIN_BLOCK, cs], ks_ref[r0:r0 + 2 * WIN_BLOCK, cs])
            mask = band_first if r0 % sub_len == 0 else band
            scores.append(jnp.where(mask, s, -jnp.inf))
        s_all = jnp.concatenate(scores, axis=0)
        m = jnp.max(s_all, axis=-1, keepdims=True)
        p = jnp.exp(s_all - m)
        den = jnp.sum(p, axis=-1, keepdims=True)
        p = p.astype(BF16)
        inv = 1.0 / den
        lse_parts = _split3(m + jnp.log(den))
        for ui, (r0, hh) in enumerate(units):
            cs = slice(hh * HEAD_DIM, (hh + 1) * HEAD_DIM)
            us = slice(ui * WIN_BLOCK, (ui + 1) * WIN_BLOCK)
            pv = _dot(p[us], vs_ref[r0:r0 + 2 * WIN_BLOCK, cs])
            o_dst[r0:r0 + WIN_BLOCK, cs] = (pv * inv[us]).astype(BF16)
            if hh == 0:
                lse_tile = jnp.zeros((WIN_BLOCK, HEAD_DIM), F32)
            for part, lsplit in enumerate(lse_parts):
                lse_tile = jnp.where(lane == part * HEADS_PER_GROUP + hh, lsplit[us], lse_tile)
            if hh == HEADS_PER_GROUP - 1:
                l_dst[r0:r0 + WIN_BLOCK, :] = lse_tile.astype(BF16)

    if dil > 1:
        perm_t = _residue_permutation(dil, transpose=True)
        for nb in range(seq // PERM_BLOCK):
            rows = slice(nb * PERM_BLOCK, (nb + 1) * PERM_BLOCK)
            pieces = [slice(r * sub_len + nb * n, r * sub_len + (nb + 1) * n) for r in range(dil)]
            o_blk = jnp.concatenate([os_ref[pc, :] for pc in pieces], axis=0)
            o_ref[rows, :] = _dot(perm_t, o_blk).astype(BF16)
            l_blk = jnp.concatenate([ls_ref[pc, :] for pc in pieces], axis=0)
            lse_ref[rows, :] = _dot(perm_t, l_blk).astype(BF16)


def _dilated_attention(qk, vb, *, bsz, seq, group, rows_per_batch=512):
    tokens = qk.shape[0]
    dil = DIL_PATTERNS[group][1]
    kern = functools.partial(_dilated_kernel, dil=dil, rows_per_batch=rows_per_batch)
    scratch = [
        pltpu.VMEM((seq, GROUP_W), BF16),
        pltpu.VMEM((seq + WIN_BLOCK, GROUP_W), BF16),
        pltpu.VMEM((seq + WIN_BLOCK, GROUP_W), BF16),
        pltpu.VMEM((seq, GROUP_W), BF16),
        pltpu.VMEM((seq, HEAD_DIM), BF16),
    ]
    return pl.pallas_call(
        kern,
        grid=(bsz,),
        in_specs=[
            pl.BlockSpec((seq, GROUP_W), lambda b: (b, group)),
            pl.BlockSpec((seq, GROUP_W), lambda b: (b, N_GROUPS + group)),
            pl.BlockSpec((seq, GROUP_W), lambda b: (b, group)),
        ],
        out_specs=[
            pl.BlockSpec((seq, GROUP_W), lambda b: (b, 0)),
            pl.BlockSpec((seq, HEAD_DIM), lambda b: (b, 0)),
        ],
        out_shape=[
            jax.ShapeDtypeStruct((tokens, GROUP_W), BF16),
            jax.ShapeDtypeStruct((tokens, HEAD_DIM), BF16),
        ],
        scratch_shapes=scratch,
        compiler_params=_params("parallel"),
        name=f"dilated_attention_g{group}",
    )(qk, qk, vb)


def _sb_kernel(q_ref, k_ref, v_ref, o_ref, *, tq, tk):
    qi = pl.program_id(1)
    heads = q_ref.shape[1] // HEAD_DIM
    diag_blocks = tq // tk
    m = heads * tq
    head_cols = [slice(hh * HEAD_DIM, (hh + 1) * HEAD_DIM) for hh in range(heads)]
    head_rows = [slice(hh * tq, (hh + 1) * tq) for hh in range(heads)]
    r = lax.broadcasted_iota(jnp.int32, (tk, tk), 0)
    c = lax.broadcasted_iota(jnp.int32, (tk, tk), 1)
    suffix = (r > c).astype(BF16)
    q_row = lax.broadcasted_iota(jnp.int32, (m, tk), 0) % tq
    k_col = lax.broadcasted_iota(jnp.int32, (m, tk), 1)

    def block(kblk, diag_offset):
        rows = pl.ds(pl.multiple_of(kblk * tk, tk), tk)
        z = jnp.concatenate([_dot_nt(q_ref[:, cs], k_ref[rows, cs]) for cs in head_cols],
                            axis=0)
        sp = jnp.log(1.0 + jnp.exp(-jnp.abs(z))) + jnp.maximum(z, 0.0)
        log_beta = z - sp
        if diag_offset is not None:
            causal = k_col + diag_offset < q_row
            sp = jnp.where(causal, sp, 0.0)
        local = _dot(sp.astype(BF16), suffix)
        a = jnp.exp(log_beta - local)
        if diag_offset is not None:
            a = jnp.where(causal, a, 0.0)
        a = a.astype(BF16)
        pv = jnp.concatenate([_dot(a[rs], v_ref[rows, cs])
                              for rs, cs in zip(head_rows, head_cols)], axis=0)
        return pv, local[:, 0:1] + sp[:, 0:1]

    def absorb(acc, later, pv, total):
        return acc + jnp.exp(-later) * pv, later + total

    acc = jnp.zeros((m, HEAD_DIM), F32)
    later = jnp.zeros((m, 1), F32)
    for d in reversed(range(diag_blocks)):
        acc, later = absorb(acc, later, *block(qi * diag_blocks + d, d * tk))

    def below_diagonal(t, carry):
        acc, later = carry
        for d in range(diag_blocks):
            acc, later = absorb(acc, later, *block((qi - t) * diag_blocks - 1 - d, None))
        return acc, later

    acc, later = lax.fori_loop(0, qi, below_diagonal, (acc, later))
    for rs, cs in zip(head_rows, head_cols):
        o_ref[:, cs] = acc[rs].astype(BF16)


def _stick_breaking(vb, *, bsz, seq, tq=512, tk=256):
    tokens = vb.shape[0]
    nq = seq // tq
    kern = functools.partial(_sb_kernel, tq=tq, tk=tk)
    return pl.pallas_call(
        kern,
        grid=(bsz, nq),
        in_specs=[
            pl.BlockSpec((tq, GROUP_W), lambda b, i: (b * nq + i, N_GROUPS)),
            pl.BlockSpec((seq, GROUP_W), lambda b, i: (b, N_GROUPS + 1)),
            pl.BlockSpec((seq, GROUP_W), lambda b, i: (b, N_GROUPS + 2)),
        ],
        out_specs=pl.BlockSpec((tq, GROUP_W), lambda b, i: (b * nq + i, 0)),
        out_shape=jax.ShapeDtypeStruct((tokens, GROUP_W), BF16),
        compiler_params=_params("parallel", "arbitrary"),
        name="stick_breaking",
    )(vb, vb, vb)


def _merge_kernel(h_ref, o0_ref, o1_ref, o2_ref, l0_ref, l1_ref, l2_ref, ob_ref,
                  sga_ref, sgb_ref, gate_ref, g2_ref, scale2_ref, shift2_ref,
                  wa_ref, wb_ref, wo_ref, out_ref, u2_ref, oa_ref):
    lses = []
    for l_ref in (l0_ref, l1_ref, l2_ref):
        lsplit = l_ref[...].astype(F32)
        lses.append([lsplit[:, hh:hh + 1] + lsplit[:, 4 + hh:5 + hh] + lsplit[:, 8 + hh:9 + hh]
                     for hh in range(HEADS_PER_GROUP)])
    for hh in range(HEADS_PER_GROUP):
        cs = slice(hh * HEAD_DIM, (hh + 1) * HEAD_DIM)
        l0, l1, l2 = lses[0][hh], lses[1][hh], lses[2][hh]
        m = jnp.maximum(jnp.maximum(l0, l1), l2)
        e0 = jnp.exp(l0 - m)
        e1 = jnp.exp(l1 - m)
        e2 = jnp.exp(l2 - m)
        inv = 1.0 / (e0 + e1 + e2)
        oa = ((e0 * inv) * o0_ref[:, cs].astype(F32)
              + (e1 * inv) * o1_ref[:, cs].astype(F32)
              + (e2 * inv) * o2_ref[:, cs].astype(F32))
        oa_ref[:, cs] = oa.astype(BF16)
    y_a = _dot(oa_ref[...], wa_ref[...])
    y_b = _dot(ob_ref[...], wb_ref[...])
    merged = sga_ref[...].astype(F32) * y_a + sgb_ref[...].astype(F32) * y_b
    h_new = h_ref[...] + gate_ref[...] * _dot(merged.astype(BF16), wo_ref[...])
    out_ref[...] = h_new
    u2_ref[...] = _modulated_norm(h_new, g2_ref[...], scale2_ref[...],
                                  shift2_ref[...]).astype(BF16)


def _merge(h, o_groups, lse_groups, o_b, gates, mod3, gate_idx, g2, shift2_idx, scale2_idx,
           wa, wb, wo, layer, *, seq, tm=512):
    tokens, d_model = h.shape
    blocks_per_seq = seq // tm
    row = lambda i: (i, 0)
    resident = lambda w: pl.BlockSpec((None,) + w.shape[1:], lambda i: (layer, 0, 0),
                                      pipeline_mode=pl.Buffered(1))
    mod_spec = lambda idx: pl.BlockSpec((None, 1, d_model),
                                        lambda i: (i // blocks_per_seq, 0, idx))
    return pl.pallas_call(
        _merge_kernel,
        grid=(tokens // tm,),
        in_specs=[
            pl.BlockSpec((tm, d_model), row),
            pl.BlockSpec((tm, GROUP_W), row),
            pl.BlockSpec((tm, GROUP_W), row),
            pl.BlockSpec((tm, GROUP_W), row),
            pl.BlockSpec((tm, HEAD_DIM), row),
            pl.BlockSpec((tm, HEAD_DIM), row),
            pl.BlockSpec((tm, HEAD_DIM), row),
            pl.BlockSpec((tm, GROUP_W), row),
            pl.BlockSpec((tm, d_model), lambda i: (i, 0)),
            pl.BlockSpec((tm, d_model), lambda i: (i, 1)),
            mod_spec(gate_idx),
            pl.BlockSpec((1, d_model), lambda i: (0, 0)),
            mod_spec(scale2_idx),
            mod_spec(shift2_idx),
            resident(wa),
            resident(wb),
            resident(wo),
        ],
        out_specs=[pl.BlockSpec((tm, d_model), row), pl.BlockSpec((tm, d_model), row)],
        out_shape=[jax.ShapeDtypeStruct((tokens, d_model), F32),
                   jax.ShapeDtypeStruct((tokens, d_model), BF16)],
        scratch_shapes=[pltpu.VMEM((tm, GROUP_W), BF16)],
        compiler_params=_params("parallel"),
        name="merge",
    )(h, *o_groups, *lse_groups, o_b, gates, gates, mod3, g2, mod3, mod3, wa, wb, wo)


def _ffn_kernel(u_ref, hres_ref, gate_ref, wg_ref, wu_ref, wd_ref, out_ref, act_ref, *, nf, tf):
    s = pl.program_id(1)

    @pl.when(s < nf)
    def _():
        u = u_ref[...]
        gate_act = _dot(u, wg_ref[...])
        up = _dot(u, wu_ref[...])
        cols = pl.ds(pl.multiple_of(s * tf, tf), tf)
        act_ref[:, cols] = (gate_act * jax.nn.sigmoid(gate_act) * up).astype(BF16)

    @pl.when(s >= nf)
    def _():
        out_ref[...] = hres_ref[...] + gate_ref[...] * _dot(act_ref[...], wd_ref[...])


def _ffn(h, u2, mod3, gate_idx, w_gate_up, w_down, layer, *, seq, tm=1024, tf=512, tn=256):
    tokens, d_model = h.shape
    d_ff = w_down.shape[1]
    nf = d_ff // tf
    nd = d_model // tn
    blocks_per_seq = seq // tm
    up_step = lambda s: jnp.minimum(s, nf - 1)
    down_step = lambda s: jnp.clip(s - nf, 0, nd - 1)
    kern = functools.partial(_ffn_kernel, nf=nf, tf=tf)
    return pl.pallas_call(
        kern,
        grid=(tokens // tm, nf + nd),
        in_specs=[
            pl.BlockSpec((tm, d_model), lambda i, s: (i, 0)),
            pl.BlockSpec((tm, tn), lambda i, s: (i, down_step(s))),
            pl.BlockSpec((None, 1, tn),
                         lambda i, s: (i // blocks_per_seq, 0, gate_idx * nd + down_step(s))),
            pl.BlockSpec((None, d_model, tf), lambda i, s: (layer, 0, up_step(s))),
            pl.BlockSpec((None, d_model, tf), lambda i, s: (layer, 0, nf + up_step(s))),
            pl.BlockSpec((None, d_ff, tn), lambda i, s: (layer, 0, down_step(s))),
        ],
        out_specs=pl.BlockSpec((tm, tn), lambda i, s: (i, down_step(s))),
        out_shape=jax.ShapeDtypeStruct((tokens, d_model), F32),
        scratch_shapes=[pltpu.VMEM((tm, d_ff), BF16)],
        compiler_params=_params("parallel", "arbitrary"),
        name="ffn",
    )(u2, h, mod3, w_gate_up, w_gate_up, w_down)


def _rope_tables(seq):
    inv = jnp.power(ROPE_THETA, -jnp.arange(0, HEAD_DIM, 2, dtype=F32) / HEAD_DIM)
    ang = jnp.arange(seq, dtype=F32)[:, None] * inv[None, :]
    cos, sin = jnp.cos(ang), jnp.sin(ang)
    return jnp.concatenate([cos, cos], axis=-1), jnp.concatenate([-sin, sin], axis=-1)


def kernel(x, c, w_ada, b_ada, norm1_g, norm2_g, w_in, qn_g, kn_g, w_branch_a, w_branch_b,
           w_out, w_gate_up, w_down):
    bsz, seq, d_model = x.shape
    depth = w_ada.shape[0]
    tokens = bsz * seq
    cos_full, sin_full = _rope_tables(seq)
    mod = _modulation(c, w_ada, b_ada)
    w_in, w_branch_a, w_branch_b, w_out, w_gate_up, w_down = (
        w.astype(BF16) for w in (w_in, w_branch_a, w_branch_b, w_out, w_gate_up, w_down))
    h = x.reshape(tokens, d_model)
    for l in range(depth):
        mod3 = mod[l].reshape(bsz, 1, 6 * d_model)
        u = _norm_modulate(h, norm1_g[l].reshape(1, d_model), mod3, 0, 1, seq)
        qk, vb, gates = _input_projections(u, w_in, l, cos_full, sin_full, qn_g[l], kn_g[l],
                                           seq=seq)
        outs = [_dilated_attention(qk, vb, bsz=bsz, seq=seq, group=g) for g in range(N_GROUPS)]
        o_b = _stick_breaking(vb, bsz=bsz, seq=seq)
        h, u2 = _merge(h, [o for o, _ in outs], [s for _, s in outs], o_b, gates, mod3, 2,
                       norm2_g[l].reshape(1, d_model), 3, 4,
                       w_branch_a, w_branch_b, w_out, l, seq=seq)
        h = _ffn(h, u2, mod3, 5, w_gate_up, w_down, l, seq=seq)
    return h.reshape(bsz, seq, d_model)
```

```python
import functools

import jax
import jax.numpy as jnp
from jax import lax
from jax.experimental import pallas as pl
from jax.experimental.pallas import tpu as pltpu

F32 = jnp.float32
BF16 = jnp.bfloat16

HEAD_DIM = 128
HEADS_PER_GROUP = 4
GROUP_W = HEADS_PER_GROUP * HEAD_DIM
DIL_PATTERNS = ((128, 1), (512, 4), (2048, 16))
N_GROUPS = len(DIL_PATTERNS)
WIN_BLOCK = 128
PERM_BLOCK = 256
BF16_SUBLANES = 16
ROPE_THETA = 10000.0
EPS = 1e-6
QK_SCALE = HEAD_DIM ** -0.5

VMEM_LIMIT_BYTES = 56 * 1024 * 1024


def _params(*semantics):
    return pltpu.CompilerParams(dimension_semantics=semantics,
                                vmem_limit_bytes=VMEM_LIMIT_BYTES)


def _dot(a, b):
    return jnp.dot(a, b, preferred_element_type=F32)


def _dot_nt(a, b):
    return lax.dot_general(a, b, (((1,), (1,)), ((), ())), preferred_element_type=F32)


def _mod_kernel(c_ref, w_ref, b_ref, o_ref):
    c = c_ref[...]
    c_act = (c * jax.nn.sigmoid(c)).astype(BF16)
    o_ref[...] = _dot(c_act, w_ref[...].astype(BF16)) + b_ref[...]


def _modulation(c, w_ada, b_ada, tn=512):
    depth, d_model, n = w_ada.shape
    bsz = c.shape[0]
    return pl.pallas_call(
        _mod_kernel,
        grid=(depth, n // tn),
        in_specs=[
            pl.BlockSpec((bsz, d_model), lambda l, j: (0, 0)),
            pl.BlockSpec((None, d_model, tn), lambda l, j: (l, 0, j)),
            pl.BlockSpec((None, 1, tn), lambda l, j: (l, 0, j)),
        ],
        out_specs=pl.BlockSpec((None, bsz, tn), lambda l, j: (l, 0, j)),
        out_shape=jax.ShapeDtypeStruct((depth, bsz, n), F32),
        compiler_params=_params("parallel", "parallel"),
        name="modulation",
    )(c, w_ada, b_ada.reshape(depth, 1, n))


def _modulated_norm(h, g, scale, shift):
    ms = jnp.mean(h * h, axis=-1, keepdims=True)
    return (h * lax.rsqrt(ms + EPS)) * g * (1.0 + scale) + shift


def _norm_kernel(h_ref, g_ref, scale_ref, shift_ref, o_ref):
    o_ref[...] = _modulated_norm(h_ref[...], g_ref[...], scale_ref[...],
                                 shift_ref[...]).astype(BF16)


def _norm_modulate(h, g, mod3, shift_idx, scale_idx, seq, tm=512):
    tokens, d_model = h.shape
    blocks_per_seq = seq // tm
    return pl.pallas_call(
        _norm_kernel,
        grid=(tokens // tm,),
        in_specs=[
            pl.BlockSpec((tm, d_model), lambda i: (i, 0)),
            pl.BlockSpec((1, d_model), lambda i: (0, 0)),
            pl.BlockSpec((None, 1, d_model), lambda i: (i // blocks_per_seq, 0, scale_idx)),
            pl.BlockSpec((None, 1, d_model), lambda i: (i // blocks_per_seq, 0, shift_idx)),
        ],
        out_specs=pl.BlockSpec((tm, d_model), lambda i: (i, 0)),
        out_shape=jax.ShapeDtypeStruct((tokens, d_model), BF16),
        compiler_params=_params("parallel"),
        name="norm_modulate",
    )(h, g, mod3, mod3)


def _rope_epilogue(acc, cos_ref, sin_ref, gain_ref, o_ref):
    cos = cos_ref[...]
    sin = sin_ref[...]
    for hh in range(acc.shape[1] // HEAD_DIM):
        cs = slice(hh * HEAD_DIM, (hh + 1) * HEAD_DIM)
        x = acc[:, cs]
        ms = jnp.mean(x * x, axis=-1, keepdims=True)
        y = (x * lax.rsqrt(ms + EPS)) * gain_ref[:, cs]
        o_ref[:, cs] = (y * cos + pltpu.roll(y, HEAD_DIM // 2, 1) * sin).astype(BF16)


def _scaled_epilogue(acc, mult_ref, o_ref):
    o_ref[...] = (acc * mult_ref[...]).astype(BF16)


def _sigmoid_epilogue(acc, o_ref):
    o_ref[...] = jax.nn.sigmoid(acc).astype(BF16)


def _cast_specs(casts, n_steps, step_of):
    in_specs, out_specs, out_shapes = [], [], []
    for w, layer in casts:
        _, n_rows, n_cols = w.shape
        rows = next(r for r in range(BF16_SUBLANES, n_rows + 1, BF16_SUBLANES)
                    if n_rows % r == 0 and n_rows // r <= n_steps)
        chunk = lambda *ids, last=n_rows // rows - 1: jnp.minimum(step_of(*ids), last)
        in_specs.append(pl.BlockSpec((None, rows, n_cols),
                                     lambda *ids, layer=layer, chunk=chunk: (layer, chunk(*ids), 0)))
        out_specs.append(pl.BlockSpec((rows, n_cols), lambda *ids, chunk=chunk: (chunk(*ids), 0)))
        out_shapes.append(jax.ShapeDtypeStruct((n_rows, n_cols), BF16))
    return in_specs, out_specs, out_shapes


def _run_casts(src_refs, dst_refs):
    for src, dst in zip(src_refs, dst_refs):
        dst[...] = src[...].astype(BF16)


def _proj_kernel(u_ref, w_ref, *refs, epilogue, n_cast):
    n_extra = len(refs) - 2 * n_cast - 2
    extra_refs, cast_in = refs[:n_extra], refs[n_extra:n_extra + n_cast]
    o_ref, cast_out, acc_ref = refs[n_extra + n_cast], refs[n_extra + n_cast + 1:-1], refs[-1]

    @pl.when(pl.program_id(0) == 0)
    def _():
        acc_ref[...] = jnp.zeros(acc_ref.shape, F32)

    epilogue(acc_ref[...], *extra_refs, o_ref)
    acc_ref[...] = _dot(u_ref[...], w_ref[...])
    _run_casts(cast_in, cast_out)


def _projection(epilogue, name, u, w, first_col, n_cols, tile_w, extra_specs, extra_args,
                casts=(), tm=1024):
    tokens, d_model = u.shape
    n_tiles = n_cols // tile_w
    first_tile = first_col // tile_w
    n_pairs = (tokens // tm) * n_tiles
    mm = lambda t: jnp.minimum(t, n_pairs - 1)
    ep = lambda t: jnp.maximum(t - 1, 0)
    lagged = lambda spec: pl.BlockSpec(
        spec.block_shape, lambda t, f=spec.index_map: f(ep(t) // n_tiles, ep(t) % n_tiles))
    cast_in, cast_out, cast_shapes = _cast_specs(casts, n_pairs + 1, lambda t: t)
    out, *converted = pl.pallas_call(
        functools.partial(_proj_kernel, epilogue=epilogue, n_cast=len(casts)),
        grid=(n_pairs + 1,),
        in_specs=[
            pl.BlockSpec((tm, d_model), lambda t: (mm(t) // n_tiles, 0)),
            pl.BlockSpec((d_model, tile_w), lambda t: (0, first_tile + mm(t) % n_tiles)),
        ] + [lagged(spec) for spec in extra_specs] + cast_in,
        out_specs=[lagged(pl.BlockSpec((tm, tile_w), lambda i, j: (i, j)))] + cast_out,
        out_shape=[jax.ShapeDtypeStruct((tokens, n_cols), BF16)] + cast_shapes,
        scratch_shapes=[pltpu.VMEM((tm, tile_w), F32)],
        compiler_params=_params("arbitrary"),
        name=name,
    )(u, w, *extra_args, *[w_f32 for w_f32, _ in casts])
    return out, converted


def _input_projections(u, w_in, cos_full, sin_full, qn_g, kn_g, casts, *, seq, tm=1024):
    d_model = u.shape[1]
    blocks_per_seq = seq // tm
    a_w = N_GROUPS * GROUP_W
    col_spec = pl.BlockSpec((1, a_w), lambda i, j: (0, j))
    rope_spec = pl.BlockSpec((tm, HEAD_DIM), lambda i, j: (i % blocks_per_seq, 0))
    gains = jnp.concatenate([jnp.tile(qn_g * QK_SCALE, a_w // HEAD_DIM),
                             jnp.tile(kn_g, a_w // HEAD_DIM)]).reshape(1, 2 * a_w)
    qk, conv_qk = _projection(_rope_epilogue, "projection_qk", u, w_in, 0, 2 * a_w, a_w,
                              [rope_spec, rope_spec, col_spec], [cos_full, sin_full, gains],
                              casts[0], tm=tm)
    vb_w = a_w + 3 * GROUP_W
    mult = jnp.ones((1, vb_w), F32).at[:, a_w:a_w + GROUP_W].set(QK_SCALE)
    vb, conv_vb = _projection(_scaled_epilogue, "projection_vb", u, w_in, 2 * a_w, vb_w, a_w,
                              [col_spec], [mult], casts[1], tm=tm)
    gates, conv_gates = _projection(_sigmoid_epilogue, "projection_gates", u, w_in,
                                    2 * a_w + vb_w, 2 * d_model, d_model, [], [], casts[2],
                                    tm=tm)
    return qk, vb, gates, (conv_qk, conv_vb, conv_gates)


def _residue_permutation(dil, transpose):
    n = PERM_BLOCK // dil
    a = lax.broadcasted_iota(jnp.int32, (PERM_BLOCK, PERM_BLOCK), 1 if transpose else 0)
    b = lax.broadcasted_iota(jnp.int32, (PERM_BLOCK, PERM_BLOCK), 0 if transpose else 1)
    return (b == (a % n) * dil + a // n).astype(BF16)


def _split3(x):
    h1 = x.astype(BF16).astype(F32)
    r1 = x - h1
    h2 = r1.astype(BF16).astype(F32)
    h3 = (r1 - h2).astype(BF16).astype(F32)
    return h1, h2, h3


def _dilated_kernel(q_ref, k_ref, v_ref, o_ref, lse_ref, qs_ref, ks_ref, vs_ref, os_ref, ls_ref,
                    *, dil, rows_per_batch):
    seq = q_ref.shape[0]
    sub_len = seq // dil
    n = PERM_BLOCK // dil
    pad = WIN_BLOCK

    zeros = jnp.zeros((pad, GROUP_W), BF16)
    ks_ref[0:pad, :] = zeros
    vs_ref[0:pad, :] = zeros
    if dil == 1:
        ks_ref[pad:, :] = k_ref[...]
        vs_ref[pad:, :] = v_ref[...]
        q_src = q_ref
    else:
        perm = _residue_permutation(dil, transpose=False)
        for src, dst, off in ((q_ref, qs_ref, 0), (k_ref, ks_ref, pad), (v_ref, vs_ref, pad)):
            for nb in range(seq // PERM_BLOCK):
                y = _dot(perm, src[nb * PERM_BLOCK:(nb + 1) * PERM_BLOCK, :]).astype(BF16)
                for r in range(dil):
                    d0 = off + r * sub_len + nb * n
                    dst[d0:d0 + n, :] = y[r * n:(r + 1) * n, :]
        q_src = qs_ref
    o_dst, l_dst = (o_ref, lse_ref) if dil == 1 else (os_ref, ls_ref)

    qi_ = lax.broadcasted_iota(jnp.int32, (WIN_BLOCK, 2 * WIN_BLOCK), 0)
    kj_ = lax.broadcasted_iota(jnp.int32, (WIN_BLOCK, 2 * WIN_BLOCK), 1)
    band = (kj_ >= qi_) & (kj_ <= qi_ + WIN_BLOCK)
    band_first = band & (kj_ >= WIN_BLOCK)
    lane = lax.broadcasted_iota(jnp.int32, (WIN_BLOCK, HEAD_DIM), 1)
    n_qb = rows_per_batch // WIN_BLOCK
    for b0 in range(0, seq, rows_per_batch):
        units = [(b0 + qb * WIN_BLOCK, hh) for qb in range(n_qb) for hh in range(HEADS_PER_GROUP)]
        scores = []
        for r0, hh in units:
            cs = slice(hh * HEAD_DIM, (hh + 1) * HEAD_DIM)
            s = _dot_nt(q_src[r0:r0 + WIN_BLOCK, cs], ks_ref[r0:r0 + 2 * WIN_BLOCK, cs])
            mask = band_first if r0 % sub_len == 0 else band
            scores.append(jnp.where(mask, s, -jnp.inf))
        s_all = jnp.concatenate(scores, axis=0)
        m = jnp.max(s_all, axis=-1, keepdims=True)
        p = jnp.exp(s_all - m)
        den = jnp.sum(p, axis=-1, keepdims=True)
        p = p.astype(BF16)
        inv = 1.0 / den
        lse_parts = _split3(m + jnp.log(den))
        for ui, (r0, hh) in enumerate(units):
            cs = slice(hh * HEAD_DIM, (hh + 1) * HEAD_DIM)
            us = slice(ui * WIN_BLOCK, (ui + 1) * WIN_BLOCK)
            pv = _dot(p[us], vs_ref[r0:r0 + 2 * WIN_BLOCK, cs])
            o_dst[r0:r0 + WIN_BLOCK, cs] = (pv * inv[us]).astype(BF16)
            if hh == 0:
                lse_tile = jnp.zeros((WIN_BLOCK, HEAD_DIM), F32)
            for part, lsplit in enumerate(lse_parts):
                lse_tile = jnp.where(lane == part * HEADS_PER_GROUP + hh, lsplit[us], lse_tile)
            if hh == HEADS_PER_GROUP - 1:
                l_dst[r0:r0 + WIN_BLOCK, :] = lse_tile.astype(BF16)

    if dil > 1:
        perm_t = _residue_permutation(dil, transpose=True)
        for nb in range(seq // PERM_BLOCK):
            rows = slice(nb * PERM_BLOCK, (nb + 1) * PERM_BLOCK)
            pieces = [slice(r * sub_len + nb * n, r * sub_len + (nb + 1) * n) for r in range(dil)]
            o_blk = jnp.concatenate([os_ref[pc, :] for pc in pieces], axis=0)
            o_ref[rows, :] = _dot(perm_t, o_blk).astype(BF16)
            l_blk = jnp.concatenate([ls_ref[pc, :] for pc in pieces], axis=0)
            lse_ref[rows, :] = _dot(perm_t, l_blk).astype(BF16)


def _dilated_attention(qk, vb, *, bsz, seq, group, rows_per_batch=512):
    tokens = qk.shape[0]
    dil = DIL_PATTERNS[group][1]
    kern = functools.partial(_dilated_kernel, dil=dil, rows_per_batch=rows_per_batch)
    scratch = [
        pltpu.VMEM((seq, GROUP_W), BF16),
        pltpu.VMEM((seq + WIN_BLOCK, GROUP_W), BF16),
        pltpu.VMEM((seq + WIN_BLOCK, GROUP_W), BF16),
        pltpu.VMEM((seq, GROUP_W), BF16),
        pltpu.VMEM((seq, HEAD_DIM), BF16),
    ]
    return pl.pallas_call(
        kern,
        grid=(bsz,),
        in_specs=[
            pl.BlockSpec((seq, GROUP_W), lambda b: (b, group)),
            pl.BlockSpec((seq, GROUP_W), lambda b: (b, N_GROUPS + group)),
            pl.BlockSpec((seq, GROUP_W), lambda b: (b, group)),
        ],
        out_specs=[
            pl.BlockSpec((seq, GROUP_W), lambda b: (b, 0)),
            pl.BlockSpec((seq, HEAD_DIM), lambda b: (b, 0)),
        ],
        out_shape=[
            jax.ShapeDtypeStruct((tokens, GROUP_W), BF16),
            jax.ShapeDtypeStruct((tokens, HEAD_DIM), BF16),
        ],
        scratch_shapes=scratch,
        compiler_params=_params("parallel"),
        name=f"dilated_attention_g{group}",
    )(qk, qk, vb)


def _sb_kernel(q_ref, k_ref, v_ref, o_ref, *, tq, tk):
    qi = pl.program_id(1)
    heads = q_ref.shape[1] // HEAD_DIM
    diag_blocks = tq // tk
    m = heads * tq
    head_cols = [slice(hh * HEAD_DIM, (hh + 1) * HEAD_DIM) for hh in range(heads)]
    head_rows = [slice(hh * tq, (hh + 1) * tq) for hh in range(heads)]
    r = lax.broadcasted_iota(jnp.int32, (tk, tk), 0)
    c = lax.broadcasted_iota(jnp.int32, (tk, tk), 1)
    suffix = (r > c).astype(BF16)
    q_row = lax.broadcasted_iota(jnp.int32, (m, tk), 0) % tq
    k_col = lax.broadcasted_iota(jnp.int32, (m, tk), 1)

    def block(kblk, diag_offset):
        rows = pl.ds(pl.multiple_of(kblk * tk, tk), tk)
        z = jnp.concatenate([_dot_nt(q_ref[:, cs], k_ref[rows, cs]) for cs in head_cols],
                            axis=0)
        sp = jnp.log(1.0 + jnp.exp(-jnp.abs(z))) + jnp.maximum(z, 0.0)
        log_beta = z - sp
        if diag_offset is not None:
            causal = k_col + diag_offset < q_row
            sp = jnp.where(causal, sp, 0.0)
        local = _dot(sp.astype(BF16), suffix)
        a = jnp.exp(log_beta - local)
        if diag_offset is not None:
            a = jnp.where(causal, a, 0.0)
        a = a.astype(BF16)
        pv = jnp.concatenate([_dot(a[rs], v_ref[rows, cs])
                              for rs, cs in zip(head_rows, head_cols)], axis=0)
        return pv, local[:, 0:1] + sp[:, 0:1]

    def absorb(acc, later, pv, total):
        return acc + jnp.exp(-later) * pv, later + total

    acc = jnp.zeros((m, HEAD_DIM), F32)
    later = jnp.zeros((m, 1), F32)
    for d in reversed(range(diag_blocks)):
        acc, later = absorb(acc, later, *block(qi * diag_blocks + d, d * tk))

    def below_diagonal(t, carry):
        acc, later = carry
        for d in range(diag_blocks):
            acc, later = absorb(acc, later, *block((qi - t) * diag_blocks - 1 - d, None))
        return acc, later

    acc, later = lax.fori_loop(0, qi, below_diagonal, (acc, later))
    for rs, cs in zip(head_rows, head_cols):
        o_ref[:, cs] = acc[rs].astype(BF16)


def _stick_breaking(vb, *, bsz, seq, tq=512, tk=256):
    tokens = vb.shape[0]
    nq = seq // tq
    kern = functools.partial(_sb_kernel, tq=tq, tk=tk)
    return pl.pallas_call(
        kern,
        grid=(bsz, nq),
        in_specs=[
            pl.BlockSpec((tq, GROUP_W), lambda b, i: (b * nq + i, N_GROUPS)),
            pl.BlockSpec((seq, GROUP_W), lambda b, i: (b, N_GROUPS + 1)),
            pl.BlockSpec((seq, GROUP_W), lambda b, i: (b, N_GROUPS + 2)),
        ],
        out_specs=pl.BlockSpec((tq, GROUP_W), lambda b, i: (b * nq + i, 0)),
        out_shape=jax.ShapeDtypeStruct((tokens, GROUP_W), BF16),
        compiler_params=_params("parallel", "arbitrary"),
        name="stick_breaking",
    )(vb, vb, vb)


def _merge_kernel(h_ref, o0_ref, o1_ref, o2_ref, l0_ref, l1_ref, l2_ref, ob_ref,
                  sga_ref, sgb_ref, gate_ref, g2_ref, scale2_ref, shift2_ref,
                  wa_ref, wb_ref, wo_ref, out_ref, u2_ref, oa_ref):
    lses = []
    for l_ref in (l0_ref, l1_ref, l2_ref):
        lsplit = l_ref[...].astype(F32)
        lses.append([lsplit[:, hh:hh + 1] + lsplit[:, 4 + hh:5 + hh] + lsplit[:, 8 + hh:9 + hh]
                     for hh in range(HEADS_PER_GROUP)])
    for hh in range(HEADS_PER_GROUP):
        cs = slice(hh * HEAD_DIM, (hh + 1) * HEAD_DIM)
        l0, l1, l2 = lses[0][hh], lses[1][hh], lses[2][hh]
        m = jnp.maximum(jnp.maximum(l0, l1), l2)
        e0 = jnp.exp(l0 - m)
        e1 = jnp.exp(l1 - m)
        e2 = jnp.exp(l2 - m)
        inv = 1.0 / (e0 + e1 + e2)
        oa = ((e0 * inv) * o0_ref[:, cs].astype(F32)
              + (e1 * inv) * o1_ref[:, cs].astype(F32)
              + (e2 * inv) * o2_ref[:, cs].astype(F32))
        oa_ref[:, cs] = oa.astype(BF16)
    y_a = _dot(oa_ref[...], wa_ref[...])
    y_b = _dot(ob_ref[...], wb_ref[...])
    merged = sga_ref[...].astype(F32) * y_a + sgb_ref[...].astype(F32) * y_b
    h_new = h_ref[...] + gate_ref[...] * _dot(merged.astype(BF16), wo_ref[...])
    out_ref[...] = h_new
    u2_ref[...] = _modulated_norm(h_new, g2_ref[...], scale2_ref[...],
                                  shift2_ref[...]).astype(BF16)


def _merge(h, o_groups, lse_groups, o_b, gates, mod3, gate_idx, g2, shift2_idx, scale2_idx,
           wa, wb, wo, *, seq, tm=512):
    tokens, d_model = h.shape
    blocks_per_seq = seq // tm
    row = lambda i: (i, 0)
    resident = lambda w: pl.BlockSpec(w.shape, lambda i: (0, 0), pipeline_mode=pl.Buffered(1))
    mod_spec = lambda idx: pl.BlockSpec((None, 1, d_model),
                                        lambda i: (i // blocks_per_seq, 0, idx))
    return pl.pallas_call(
        _merge_kernel,
        grid=(tokens // tm,),
        in_specs=[
            pl.BlockSpec((tm, d_model), row),
            pl.BlockSpec((tm, GROUP_W), row),
            pl.BlockSpec((tm, GROUP_W), row),
            pl.BlockSpec((tm, GROUP_W), row),
            pl.BlockSpec((tm, HEAD_DIM), row),
            pl.BlockSpec((tm, HEAD_DIM), row),
            pl.BlockSpec((tm, HEAD_DIM), row),
            pl.BlockSpec((tm, GROUP_W), row),
            pl.BlockSpec((tm, d_model), lambda i: (i, 0)),
            pl.BlockSpec((tm, d_model), lambda i: (i, 1)),
            mod_spec(gate_idx),
            pl.BlockSpec((1, d_model), lambda i: (0, 0)),
            mod_spec(scale2_idx),
            mod_spec(shift2_idx),
            resident(wa),
            resident(wb),
            resident(wo),
        ],
        out_specs=[pl.BlockSpec((tm, d_model), row), pl.BlockSpec((tm, d_model), row)],
        out_shape=[jax.ShapeDtypeStruct((tokens, d_model), F32),
                   jax.ShapeDtypeStruct((tokens, d_model), BF16)],
        scratch_shapes=[pltpu.VMEM((tm, GROUP_W), BF16)],
        compiler_params=_params("parallel"),
        name="merge",
    )(h, *o_groups, *lse_groups, o_b, gates, gates, mod3, g2, mod3, mod3, wa, wb, wo)


def _ffn_kernel(u_ref, hres_ref, gate_ref, wg_ref, wu_ref, wd_ref, *refs, nf, tf, n_cast):
    cast_in, out_ref, cast_out, act_ref = refs[:n_cast], refs[n_cast], refs[n_cast + 1:-1], refs[-1]
    s = pl.program_id(1)

    @pl.when(s < nf)
    def _():
        u = u_ref[...]
        gate_act = _dot(u, wg_ref[...])
        up = _dot(u, wu_ref[...])
        cols = pl.ds(pl.multiple_of(s * tf, tf), tf)
        act_ref[:, cols] = (gate_act * jax.nn.sigmoid(gate_act) * up).astype(BF16)
        _run_casts(cast_in, cast_out)

    @pl.when(s >= nf)
    def _():
        out_ref[...] = hres_ref[...] + gate_ref[...] * _dot(act_ref[...], wd_ref[...])
        _run_casts(cast_in, cast_out)


def _ffn(h, u2, mod3, gate_idx, w_gate_up, w_down, casts=(), *, seq, tm=1024, tf=512, tn=512):
    tokens, d_model = h.shape
    d_ff = w_down.shape[0]
    nf = d_ff // tf
    nd = d_model // tn
    steps = nf + nd
    blocks_per_seq = seq // tm
    up_step = lambda s: jnp.minimum(s, nf - 1)
    down_step = lambda s: jnp.clip(s - nf, 0, nd - 1)
    cast_in, cast_out, cast_shapes = _cast_specs(casts, (tokens // tm) * steps,
                                                 lambda i, s: i * steps + s)
    kern = functools.partial(_ffn_kernel, nf=nf, tf=tf, n_cast=len(casts))
    out, *converted = pl.pallas_call(
        kern,
        grid=(tokens // tm, steps),
        in_specs=[
            pl.BlockSpec((tm, d_model), lambda i, s: (i, 0)),
            pl.BlockSpec((tm, tn), lambda i, s: (i, down_step(s))),
            pl.BlockSpec((None, 1, tn),
                         lambda i, s: (i // blocks_per_seq, 0, gate_idx * nd + down_step(s))),
            pl.BlockSpec((d_model, tf), lambda i, s: (0, up_step(s))),
            pl.BlockSpec((d_model, tf), lambda i, s: (0, nf + up_step(s))),
            pl.BlockSpec((d_ff, tn), lambda i, s: (0, down_step(s))),
        ] + cast_in,
        out_specs=[pl.BlockSpec((tm, tn), lambda i, s: (i, down_step(s)))] + cast_out,
        out_shape=[jax.ShapeDtypeStruct((tokens, d_model), F32)] + cast_shapes,
        scratch_shapes=[pltpu.VMEM((tm, d_ff), BF16)],
        compiler_params=_params("arbitrary", "arbitrary"),
        name="ffn",
    )(u2, h, mod3, w_gate_up, w_gate_up, w_down, *[w_f32 for w_f32, _ in casts])
    return out, converted


def _rope_tables(seq):
    inv = jnp.power(ROPE_THETA, -jnp.arange(0, HEAD_DIM, 2, dtype=F32) / HEAD_DIM)
    ang = jnp.arange(seq, dtype=F32)[:, None] * inv[None, :]
    cos, sin = jnp.cos(ang), jnp.sin(ang)
    return jnp.concatenate([cos, cos], axis=-1), jnp.concatenate([-sin, sin], axis=-1)


def kernel(x, c, w_ada, b_ada, norm1_g, norm2_g, w_in, qn_g, kn_g, w_branch_a, w_branch_b,
           w_out, w_gate_up, w_down):
    bsz, seq, d_model = x.shape
    depth = w_ada.shape[0]
    tokens = bsz * seq
    cos_full, sin_full = _rope_tables(seq)
    mod = _modulation(c, w_ada, b_ada)
    w_in_l = w_in[0].astype(BF16)
    h = x.reshape(tokens, d_model)
    for l in range(depth):
        mod3 = mod[l].reshape(bsz, 1, 6 * d_model)
        u = _norm_modulate(h, norm1_g[l].reshape(1, d_model), mod3, 0, 1, seq)
        if l == 0:
            casts = ([(w_gate_up, 0)], [(w_down, 0)],
                     [(w_out, 0), (w_branch_a, 0), (w_branch_b, 0)])
        else:
            casts = ([], [], [])
        qk, vb, gates, converted = _input_projections(u, w_in_l, cos_full, sin_full, qn_g[l],
                                                      kn_g[l], casts, seq=seq)
        if l == 0:
            (w_gate_up_l,), (w_down_l,), (w_out_l, w_branch_a_l, w_branch_b_l) = converted
        outs = [_dilated_attention(qk, vb, bsz=bsz, seq=seq, group=g) for g in range(N_GROUPS)]
        o_b = _stick_breaking(vb, bsz=bsz, seq=seq)
        h, u2 = _merge(h, [o for o, _ in outs], [s for _, s in outs], o_b, gates, mod3, 2,
                       norm2_g[l].reshape(1, d_model), 3, 4,
                       w_branch_a_l, w_branch_b_l, w_out_l, seq=seq)
        next_layer = [(w, l + 1) for w in (w_in, w_gate_up, w_down, w_out, w_branch_a,
                                            w_branch_b)] if l + 1 < depth else []
        h, converted = _ffn(h, u2, mod3, 5, w_gate_up_l, w_down_l, next_layer, seq=seq,
                            tn=256 if next_layer else 512)
        if converted:
            w_in_l, w_gate_up_l, w_down_l, w_out_l, w_branch_a_l, w_branch_b_l = converted
    return h.reshape(bsz, seq, d_model)
```

```python
import functools

import jax
import jax.numpy as jnp
from jax import lax
from jax.experimental import pallas as pl
from jax.experimental.pallas import tpu as pltpu

F32 = jnp.float32
BF16 = jnp.bfloat16

HEAD_DIM = 128
HEADS_PER_GROUP = 4
GROUP_W = HEADS_PER_GROUP * HEAD_DIM
DIL_PATTERNS = ((128, 1), (512, 4), (2048, 16))
N_GROUPS = len(DIL_PATTERNS)
WIN_BLOCK = 128
PERM_BLOCK = 256
BF16_SUBLANES = 16
ROPE_THETA = 10000.0
EPS = 1e-6
QK_SCALE = HEAD_DIM ** -0.5

VMEM_LIMIT_BYTES = 56 * 1024 * 1024


def _params(*semantics):
    return pltpu.CompilerParams(dimension_semantics=semantics,
                                vmem_limit_bytes=VMEM_LIMIT_BYTES)


def _dot(a, b):
    return jnp.dot(a, b, preferred_element_type=F32)


def _dot_nt(a, b):
    return lax.dot_general(a, b, (((1,), (1,)), ((), ())), preferred_element_type=F32)


def _mod_kernel(c_ref, w_ref, b_ref, o_ref):
    c = c_ref[...]
    c_act = (c * jax.nn.sigmoid(c)).astype(BF16)
    o_ref[...] = _dot(c_act, w_ref[...].astype(BF16)) + b_ref[...]


def _modulation(c, w_ada, b_ada, tn=512):
    depth, d_model, n = w_ada.shape
    bsz = c.shape[0]
    return pl.pallas_call(
        _mod_kernel,
        grid=(depth, n // tn),
        in_specs=[
            pl.BlockSpec((bsz, d_model), lambda l, j: (0, 0)),
            pl.BlockSpec((None, d_model, tn), lambda l, j: (l, 0, j)),
            pl.BlockSpec((None, 1, tn), lambda l, j: (l, 0, j)),
        ],
        out_specs=pl.BlockSpec((None, bsz, tn), lambda l, j: (l, 0, j)),
        out_shape=jax.ShapeDtypeStruct((depth, bsz, n), F32),
        compiler_params=_params("parallel", "parallel"),
        name="modulation",
    )(c, w_ada, b_ada.reshape(depth, 1, n))


def _modulated_norm(h, g, scale, shift):
    ms = jnp.mean(h * h, axis=-1, keepdims=True)
    return (h * lax.rsqrt(ms + EPS)) * (g * (1.0 + scale)) + shift


def _norm_kernel(h_ref, g_ref, scale_ref, shift_ref, o_ref):
    o_ref[...] = _modulated_norm(h_ref[...], g_ref[...], scale_ref[...],
                                 shift_ref[...]).astype(BF16)


def _norm_modulate(h, g, mod3, shift_idx, scale_idx, seq, tm=512):
    tokens, d_model = h.shape
    blocks_per_seq = seq // tm
    return pl.pallas_call(
        _norm_kernel,
        grid=(tokens // tm,),
        in_specs=[
            pl.BlockSpec((tm, d_model), lambda i: (i, 0)),
            pl.BlockSpec((1, d_model), lambda i: (0, 0)),
            pl.BlockSpec((None, 1, d_model), lambda i: (i // blocks_per_seq, 0, scale_idx)),
            pl.BlockSpec((None, 1, d_model), lambda i: (i // blocks_per_seq, 0, shift_idx)),
        ],
        out_specs=pl.BlockSpec((tm, d_model), lambda i: (i, 0)),
        out_shape=jax.ShapeDtypeStruct((tokens, d_model), BF16),
        compiler_params=_params("parallel"),
        name="norm_modulate",
    )(h, g, mod3, mod3)


def _rope_epilogue(acc, cos_ref, sin_ref, gain_ref, o_ref):
    cos = cos_ref[...]
    sin = sin_ref[...]
    for hh in range(acc.shape[1] // HEAD_DIM):
        cs = slice(hh * HEAD_DIM, (hh + 1) * HEAD_DIM)
        x = acc[:, cs]
        ms = jnp.mean(x * x, axis=-1, keepdims=True)
        y = (x * lax.rsqrt(ms + EPS)) * gain_ref[:, cs]
        o_ref[:, cs] = (y * cos + pltpu.roll(y, HEAD_DIM // 2, 1) * sin).astype(BF16)


def _scaled_epilogue(acc, mult_ref, o_ref):
    o_ref[...] = (acc * mult_ref[...]).astype(BF16)


def _sigmoid_epilogue(acc, o_ref):
    o_ref[...] = jax.nn.sigmoid(acc).astype(BF16)


def _cast_specs(casts, n_steps, step_of):
    in_specs, out_specs, out_shapes = [], [], []
    for w, layer in casts:
        _, n_rows, n_cols = w.shape
        rows = next(r for r in range(BF16_SUBLANES, n_rows + 1, BF16_SUBLANES)
                    if n_rows % r == 0 and n_rows // r <= n_steps)
        chunk = lambda *ids, last=n_rows // rows - 1: jnp.minimum(step_of(*ids), last)
        in_specs.append(pl.BlockSpec((None, rows, n_cols),
                                     lambda *ids, layer=layer, chunk=chunk: (layer, chunk(*ids), 0)))
        out_specs.append(pl.BlockSpec((rows, n_cols), lambda *ids, chunk=chunk: (chunk(*ids), 0)))
        out_shapes.append(jax.ShapeDtypeStruct((n_rows, n_cols), BF16))
    return in_specs, out_specs, out_shapes


def _run_casts(src_refs, dst_refs):
    for src, dst in zip(src_refs, dst_refs):
        dst[...] = src[...].astype(BF16)


def _proj_kernel(u_ref, w_ref, *refs, epilogue, n_cast):
    n_extra = len(refs) - 2 * n_cast - 2
    extra_refs, cast_in = refs[:n_extra], refs[n_extra:n_extra + n_cast]
    o_ref, cast_out, acc_ref = refs[n_extra + n_cast], refs[n_extra + n_cast + 1:-1], refs[-1]

    @pl.when(pl.program_id(0) == 0)
    def _():
        acc_ref[...] = jnp.zeros(acc_ref.shape, F32)

    epilogue(acc_ref[...], *extra_refs, o_ref)
    acc_ref[...] = _dot(u_ref[...], w_ref[...])
    _run_casts(cast_in, cast_out)


def _projection(epilogue, name, u, w, first_col, n_cols, tile_w, extra_specs, extra_args,
                casts=(), tm=1024):
    tokens, d_model = u.shape
    n_tiles = n_cols // tile_w
    first_tile = first_col // tile_w
    n_pairs = (tokens // tm) * n_tiles
    mm = lambda t: jnp.minimum(t, n_pairs - 1)
    ep = lambda t: jnp.maximum(t - 1, 0)
    lagged = lambda spec: pl.BlockSpec(
        spec.block_shape, lambda t, f=spec.index_map: f(ep(t) // n_tiles, ep(t) % n_tiles))
    cast_in, cast_out, cast_shapes = _cast_specs(casts, n_pairs + 1, lambda t: t)
    out, *converted = pl.pallas_call(
        functools.partial(_proj_kernel, epilogue=epilogue, n_cast=len(casts)),
        grid=(n_pairs + 1,),
        in_specs=[
            pl.BlockSpec((tm, d_model), lambda t: (mm(t) // n_tiles, 0)),
            pl.BlockSpec((d_model, tile_w), lambda t: (0, first_tile + mm(t) % n_tiles)),
        ] + [lagged(spec) for spec in extra_specs] + cast_in,
        out_specs=[lagged(pl.BlockSpec((tm, tile_w), lambda i, j: (i, j)))] + cast_out,
        out_shape=[jax.ShapeDtypeStruct((tokens, n_cols), BF16)] + cast_shapes,
        scratch_shapes=[pltpu.VMEM((tm, tile_w), F32)],
        compiler_params=_params("arbitrary"),
        name=name,
    )(u, w, *extra_args, *[w_f32 for w_f32, _ in casts])
    return out, converted


def _input_projections(u, w_in, cos_full, sin_full, qn_g, kn_g, casts, *, seq, tm=1024):
    d_model = u.shape[1]
    blocks_per_seq = seq // tm
    a_w = N_GROUPS * GROUP_W
    col_spec = pl.BlockSpec((1, a_w), lambda i, j: (0, j))
    rope_spec = pl.BlockSpec((tm, HEAD_DIM), lambda i, j: (i % blocks_per_seq, 0))
    gains = jnp.concatenate([jnp.tile(qn_g * QK_SCALE, a_w // HEAD_DIM),
                             jnp.tile(kn_g, a_w // HEAD_DIM)]).reshape(1, 2 * a_w)
    qk, conv_qk = _projection(_rope_epilogue, "projection_qk", u, w_in, 0, 2 * a_w, a_w,
                              [rope_spec, rope_spec, col_spec], [cos_full, sin_full, gains],
                              casts[0], tm=tm)
    vb_w = a_w + 3 * GROUP_W
    mult = jnp.ones((1, vb_w), F32).at[:, a_w:a_w + GROUP_W].set(QK_SCALE)
    vb, conv_vb = _projection(_scaled_epilogue, "projection_vb", u, w_in, 2 * a_w, vb_w, a_w,
                              [col_spec], [mult], casts[1], tm=tm)
    gates, conv_gates = _projection(_sigmoid_epilogue, "projection_gates", u, w_in,
                                    2 * a_w + vb_w, 2 * d_model, d_model, [], [], casts[2],
                                    tm=tm)
    return qk, vb, gates, (conv_qk, conv_vb, conv_gates)


def _residue_permutation(dil, transpose):
    n = PERM_BLOCK // dil
    a = lax.broadcasted_iota(jnp.int32, (PERM_BLOCK, PERM_BLOCK), 1 if transpose else 0)
    b = lax.broadcasted_iota(jnp.int32, (PERM_BLOCK, PERM_BLOCK), 0 if transpose else 1)
    return (b == (a % n) * dil + a // n).astype(BF16)


def _dilated_kernel(q_ref, k_ref, v_ref, o_ref, lse_ref, qs_ref, ks_ref, vs_ref, os_ref, ls_ref,
                    *, dil, rows_per_batch):
    seq = q_ref.shape[0]
    sub_len = seq // dil
    n = PERM_BLOCK // dil
    pad = WIN_BLOCK

    zeros = jnp.zeros((pad, GROUP_W), BF16)
    ks_ref[0:pad, :] = zeros
    vs_ref[0:pad, :] = zeros
    if dil == 1:
        ks_ref[pad:, :] = k_ref[...]
        vs_ref[pad:, :] = v_ref[...]
        q_src = q_ref
    else:
        perm = _residue_permutation(dil, transpose=False)
        for src, dst, off in ((q_ref, qs_ref, 0), (k_ref, ks_ref, pad), (v_ref, vs_ref, pad)):
            for nb in range(seq // PERM_BLOCK):
                y = _dot(perm, src[nb * PERM_BLOCK:(nb + 1) * PERM_BLOCK, :]).astype(BF16)
                for r in range(dil):
                    d0 = off + r * sub_len + nb * n
                    dst[d0:d0 + n, :] = y[r * n:(r + 1) * n, :]
        q_src = qs_ref
    o_dst, l_dst = (o_ref, lse_ref) if dil == 1 else (os_ref, ls_ref)

    qi_ = lax.broadcasted_iota(jnp.int32, (WIN_BLOCK, 2 * WIN_BLOCK), 0)
    kj_ = lax.broadcasted_iota(jnp.int32, (WIN_BLOCK, 2 * WIN_BLOCK), 1)
    band = (kj_ >= qi_) & (kj_ <= qi_ + WIN_BLOCK)
    band_first = band & (kj_ >= WIN_BLOCK)
    lane = lax.broadcasted_iota(jnp.int32, (WIN_BLOCK, HEAD_DIM), 1)
    n_qb = rows_per_batch // WIN_BLOCK
    for b0 in range(0, seq, rows_per_batch):
        units = [(b0 + qb * WIN_BLOCK, hh) for qb in range(n_qb) for hh in range(HEADS_PER_GROUP)]
        scores = []
        for r0, hh in units:
            cs = slice(hh * HEAD_DIM, (hh + 1) * HEAD_DIM)
            s = _dot_nt(q_src[r0:r0 + WIN_BLOCK, cs], ks_ref[r0:r0 + 2 * WIN_BLOCK, cs])
            mask = band_first if r0 % sub_len == 0 else band
            scores.append(jnp.where(mask, s, -jnp.inf))
        s_all = jnp.concatenate(scores, axis=0)
        m = jnp.max(s_all, axis=-1, keepdims=True)
        p = jnp.exp(s_all - m)
        den = jnp.sum(p, axis=-1, keepdims=True)
        p = p.astype(BF16)
        inv = 1.0 / den
        lse = m + jnp.log(den)
        for ui, (r0, hh) in enumerate(units):
            cs = slice(hh * HEAD_DIM, (hh + 1) * HEAD_DIM)
            us = slice(ui * WIN_BLOCK, (ui + 1) * WIN_BLOCK)
            pv = _dot(p[us], vs_ref[r0:r0 + 2 * WIN_BLOCK, cs])
            o_dst[r0:r0 + WIN_BLOCK, cs] = (pv * inv[us]).astype(BF16)
            if hh == 0:
                lse_tile = jnp.zeros((WIN_BLOCK, HEAD_DIM), F32)
            lse_tile = jnp.where(lane == hh, lse[us], lse_tile)
            if hh == HEADS_PER_GROUP - 1:
                l_dst[r0:r0 + WIN_BLOCK, :] = lse_tile

    if dil > 1:
        perm_t = _residue_permutation(dil, transpose=True)
        for nb in range(seq // PERM_BLOCK):
            rows = slice(nb * PERM_BLOCK, (nb + 1) * PERM_BLOCK)
            pieces = [slice(r * sub_len + nb * n, r * sub_len + (nb + 1) * n) for r in range(dil)]
            o_blk = jnp.concatenate([os_ref[pc, :] for pc in pieces], axis=0)
            o_ref[rows, :] = _dot(perm_t, o_blk).astype(BF16)
            for r, pc in enumerate(pieces):
                lse_ref[pl.ds(nb * PERM_BLOCK + r, n, stride=dil), :] = ls_ref[pc, :]


def _dilated_attention(qk, vb, *, bsz, seq, group, rows_per_batch=512):
    tokens = qk.shape[0]
    dil = DIL_PATTERNS[group][1]
    kern = functools.partial(_dilated_kernel, dil=dil, rows_per_batch=rows_per_batch)
    scratch = [
        pltpu.VMEM((seq, GROUP_W), BF16),
        pltpu.VMEM((seq + WIN_BLOCK, GROUP_W), BF16),
        pltpu.VMEM((seq + WIN_BLOCK, GROUP_W), BF16),
        pltpu.VMEM((seq, GROUP_W), BF16),
        pltpu.VMEM((seq, HEAD_DIM), F32),
    ]
    return pl.pallas_call(
        kern,
        grid=(bsz,),
        in_specs=[
            pl.BlockSpec((seq, GROUP_W), lambda b: (b, group)),
            pl.BlockSpec((seq, GROUP_W), lambda b: (b, N_GROUPS + group)),
            pl.BlockSpec((seq, GROUP_W), lambda b: (b, group)),
        ],
        out_specs=[
            pl.BlockSpec((seq, GROUP_W), lambda b: (b, 0)),
            pl.BlockSpec((seq, HEAD_DIM), lambda b: (b, 0)),
        ],
        out_shape=[
            jax.ShapeDtypeStruct((tokens, GROUP_W), BF16),
            jax.ShapeDtypeStruct((tokens, HEAD_DIM), F32),
        ],
        scratch_shapes=scratch,
        compiler_params=_params("parallel"),
        name=f"dilated_attention_g{group}",
    )(qk, qk, vb)


def _sb_kernel(q_ref, k_ref, v_ref, *refs, tq, tk, n_cast):
    cast_in, o_ref, cast_out = refs[:n_cast], refs[n_cast], refs[n_cast + 1:]
    _run_casts(cast_in, cast_out)
    qi = pl.program_id(1)
    heads = q_ref.shape[1] // HEAD_DIM
    diag_blocks = tq // tk
    m = heads * tq
    head_cols = [slice(hh * HEAD_DIM, (hh + 1) * HEAD_DIM) for hh in range(heads)]
    head_rows = [slice(hh * tq, (hh + 1) * tq) for hh in range(heads)]
    r = lax.broadcasted_iota(jnp.int32, (tk, tk), 0)
    c = lax.broadcasted_iota(jnp.int32, (tk, tk), 1)
    suffix = (r > c).astype(BF16)
    q_row = lax.broadcasted_iota(jnp.int32, (m, tk), 0) % tq
    k_col = lax.broadcasted_iota(jnp.int32, (m, tk), 1)

    def block(kblk, diag_offset):
        rows = pl.ds(pl.multiple_of(kblk * tk, tk), tk)
        z = jnp.concatenate([_dot_nt(q_ref[:, cs], k_ref[rows, cs]) for cs in head_cols],
                            axis=0)
        sp = jnp.log(1.0 + jnp.exp(-jnp.abs(z))) + jnp.maximum(z, 0.0)
        log_beta = z - sp
        if diag_offset is not None:
            causal = k_col + diag_offset < q_row
            sp = jnp.where(causal, sp, 0.0)
        local = _dot(sp.astype(BF16), suffix)
        a = jnp.exp(log_beta - local)
        if diag_offset is not None:
            a = jnp.where(causal, a, 0.0)
        a = a.astype(BF16)
        pv = jnp.concatenate([_dot(a[rs], v_ref[rows, cs])
                              for rs, cs in zip(head_rows, head_cols)], axis=0)
        return pv, local[:, 0:1] + sp[:, 0:1]

    def absorb(acc, later, pv, total):
        return acc + jnp.exp(-later) * pv, later + total

    acc = jnp.zeros((m, HEAD_DIM), F32)
    later = jnp.zeros((m, 1), F32)
    for d in reversed(range(diag_blocks)):
        acc, later = absorb(acc, later, *block(qi * diag_blocks + d, d * tk))

    def below_diagonal(t, carry):
        acc, later = carry
        for d in range(diag_blocks):
            acc, later = absorb(acc, later, *block((qi - t) * diag_blocks - 1 - d, None))
        return acc, later

    acc, later = lax.fori_loop(0, qi, below_diagonal, (acc, later))
    for rs, cs in zip(head_rows, head_cols):
        o_ref[:, cs] = acc[rs].astype(BF16)


def _stick_breaking(vb, casts=(), *, bsz, seq, tq=512, tk=256):
    tokens = vb.shape[0]
    nq = seq // tq
    cast_in, cast_out, cast_shapes = _cast_specs(casts, bsz * nq, lambda b, i: b * nq + i)
    kern = functools.partial(_sb_kernel, tq=tq, tk=tk, n_cast=len(casts))
    out, *converted = pl.pallas_call(
        kern,
        grid=(bsz, nq),
        in_specs=[
            pl.BlockSpec((tq, GROUP_W), lambda b, i: (b * nq + i, N_GROUPS)),
            pl.BlockSpec((seq, GROUP_W), lambda b, i: (b, N_GROUPS + 1)),
            pl.BlockSpec((seq, GROUP_W), lambda b, i: (b, N_GROUPS + 2)),
        ] + cast_in,
        out_specs=[pl.BlockSpec((tq, GROUP_W), lambda b, i: (b * nq + i, 0))] + cast_out,
        out_shape=[jax.ShapeDtypeStruct((tokens, GROUP_W), BF16)] + cast_shapes,
        compiler_params=_params("arbitrary", "arbitrary"),
        name="stick_breaking",
    )(vb, vb, vb, *[w_f32 for w_f32, _ in casts])
    return out, converted


def _merge_kernel(h_ref, o0_ref, o1_ref, o2_ref, l0_ref, l1_ref, l2_ref, ob_ref,
                  sga_ref, sgb_ref, gate_ref, g2_ref, scale2_ref, shift2_ref,
                  wa_ref, wb_ref, wo_ref, out_ref, u2_ref, oa_ref):
    l0, l1, l2 = l0_ref[...], l1_ref[...], l2_ref[...]
    m = jnp.maximum(jnp.maximum(l0, l1), l2)
    e0 = jnp.exp(l0 - m)
    e1 = jnp.exp(l1 - m)
    e2 = jnp.exp(l2 - m)
    inv = 1.0 / (e0 + e1 + e2)
    w0, w1, w2 = e0 * inv, e1 * inv, e2 * inv
    for hh in range(HEADS_PER_GROUP):
        cs = slice(hh * HEAD_DIM, (hh + 1) * HEAD_DIM)
        oa = (w0[:, hh:hh + 1] * o0_ref[:, cs].astype(F32)
              + w1[:, hh:hh + 1] * o1_ref[:, cs].astype(F32)
              + w2[:, hh:hh + 1] * o2_ref[:, cs].astype(F32))
        oa_ref[:, cs] = oa.astype(BF16)
    y_a = _dot(oa_ref[...], wa_ref[...])
    y_b = _dot(ob_ref[...], wb_ref[...])
    merged = sga_ref[...].astype(F32) * y_a + sgb_ref[...].astype(F32) * y_b
    h_new = h_ref[...] + gate_ref[...] * _dot(merged.astype(BF16), wo_ref[...])
    out_ref[...] = h_new
    u2_ref[...] = _modulated_norm(h_new, g2_ref[...], scale2_ref[...],
                                  shift2_ref[...]).astype(BF16)


def _merge(h, o_groups, lse_groups, o_b, gates, mod3, gate_idx, g2, shift2_idx, scale2_idx,
           wa, wb, wo, *, seq, tm=512):
    tokens, d_model = h.shape
    blocks_per_seq = seq // tm
    row = lambda i: (i, 0)
    resident = lambda w: pl.BlockSpec(w.shape, lambda i: (0, 0), pipeline_mode=pl.Buffered(1))
    mod_spec = lambda idx: pl.BlockSpec((None, 1, d_model),
                                        lambda i: (i // blocks_per_seq, 0, idx))
    return pl.pallas_call(
        _merge_kernel,
        grid=(tokens // tm,),
        in_specs=[
            pl.BlockSpec((tm, d_model), row),
            pl.BlockSpec((tm, GROUP_W), row),
            pl.BlockSpec((tm, GROUP_W), row),
            pl.BlockSpec((tm, GROUP_W), row),
            pl.BlockSpec((tm, HEAD_DIM), row),
            pl.BlockSpec((tm, HEAD_DIM), row),
            pl.BlockSpec((tm, HEAD_DIM), row),
            pl.BlockSpec((tm, GROUP_W), row),
            pl.BlockSpec((tm, d_model), lambda i: (i, 0)),
            pl.BlockSpec((tm, d_model), lambda i: (i, 1)),
            mod_spec(gate_idx),
            pl.BlockSpec((1, d_model), lambda i: (0, 0)),
            mod_spec(scale2_idx),
            mod_spec(shift2_idx),
            resident(wa),
            resident(wb),
            resident(wo),
        ],
        out_specs=[pl.BlockSpec((tm, d_model), row), pl.BlockSpec((tm, d_model), row)],
        out_shape=[jax.ShapeDtypeStruct((tokens, d_model), F32),
                   jax.ShapeDtypeStruct((tokens, d_model), BF16)],
        scratch_shapes=[pltpu.VMEM((tm, GROUP_W), BF16)],
        compiler_params=_params("parallel"),
        name="merge",
    )(h, *o_groups, *lse_groups, o_b, gates, gates, mod3, g2, mod3, mod3, wa, wb, wo)


def _ffn_kernel(u_ref, hres_ref, gate_ref, wg_ref, wu_ref, wd_ref, *refs, nf, tf, n_cast):
    cast_in, out_ref, cast_out, act_ref = refs[:n_cast], refs[n_cast], refs[n_cast + 1:-1], refs[-1]
    s = pl.program_id(1)

    @pl.when(s < nf)
    def _():
        u = u_ref[...]
        gate_act = _dot(u, wg_ref[...])
        up = _dot(u, wu_ref[...])
        cols = pl.ds(pl.multiple_of(s * tf, tf), tf)
        act_ref[:, cols] = (gate_act * jax.nn.sigmoid(gate_act) * up).astype(BF16)
        _run_casts(cast_in, cast_out)

    @pl.when(s >= nf)
    def _():
        out_ref[...] = hres_ref[...] + gate_ref[...] * _dot(act_ref[...], wd_ref[...])
        _run_casts(cast_in, cast_out)


def _ffn(h, u2, mod3, gate_idx, w_gate_up, w_down, casts=(), *, seq, tm=1024, tf=512, tn=512):
    tokens, d_model = h.shape
    d_ff = w_down.shape[0]
    nf = d_ff // tf
    nd = d_model // tn
    steps = nf + nd
    blocks_per_seq = seq // tm
    up_step = lambda s: jnp.minimum(s, nf - 1)
    down_step = lambda s: jnp.clip(s - nf, 0, nd - 1)
    cast_in, cast_out, cast_shapes = _cast_specs(casts, (tokens // tm) * steps,
                                                 lambda i, s: i * steps + s)
    kern = functools.partial(_ffn_kernel, nf=nf, tf=tf, n_cast=len(casts))
    out, *converted = pl.pallas_call(
        kern,
        grid=(tokens // tm, steps),
        in_specs=[
            pl.BlockSpec((tm, d_model), lambda i, s: (i, 0)),
            pl.BlockSpec((tm, tn), lambda i, s: (i, down_step(s))),
            pl.BlockSpec((None, 1, tn),
                         lambda i, s: (i // blocks_per_seq, 0, gate_idx * nd + down_step(s))),
            pl.BlockSpec((d_model, tf), lambda i, s: (0, up_step(s))),
            pl.BlockSpec((d_model, tf), lambda i, s: (0, nf + up_step(s))),
            pl.BlockSpec((d_ff, tn), lambda i, s: (0, down_step(s))),
        ] + cast_in,
        out_specs=[pl.BlockSpec((tm, tn), lambda i, s: (i, down_step(s)))] + cast_out,
        out_shape=[jax.ShapeDtypeStruct((tokens, d_model), F32)] + cast_shapes,
        scratch_shapes=[pltpu.VMEM((tm, d_ff), BF16)],
        compiler_params=_params("arbitrary", "arbitrary"),
        name="ffn",
    )(u2, h, mod3, w_gate_up, w_gate_up, w_down, *[w_f32 for w_f32, _ in casts])
    return out, converted


def _rope_tables(seq):
    inv = jnp.power(ROPE_THETA, -jnp.arange(0, HEAD_DIM, 2, dtype=F32) / HEAD_DIM)
    ang = jnp.arange(seq, dtype=F32)[:, None] * inv[None, :]
    cos, sin = jnp.cos(ang), jnp.sin(ang)
    return jnp.concatenate([cos, cos], axis=-1), jnp.concatenate([-sin, sin], axis=-1)


def kernel(x, c, w_ada, b_ada, norm1_g, norm2_g, w_in, qn_g, kn_g, w_branch_a, w_branch_b,
           w_out, w_gate_up, w_down):
    bsz, seq, d_model = x.shape
    depth = w_ada.shape[0]
    tokens = bsz * seq
    cos_full, sin_full = _rope_tables(seq)
    mod = _modulation(c, w_ada, b_ada)
    w_in_l = w_in[0].astype(BF16)
    h = x.reshape(tokens, d_model)
    for l in range(depth):
        mod3 = mod[l].reshape(bsz, 1, 6 * d_model)
        u = _norm_modulate(h, norm1_g[l].reshape(1, d_model), mod3, 0, 1, seq)
        has_next = l + 1 < depth
        nxt = lambda *ws: [(w, l + 1) for w in ws] if has_next else []
        first = lambda *ws: [(w, 0) for w in ws] if l == 0 else []
        casts = (first(w_gate_up), first(w_down),
                 first(w_out, w_branch_a, w_branch_b) + nxt(w_in))
        qk, vb, gates, converted = _input_projections(u, w_in_l, cos_full, sin_full, qn_g[l],
                                                      kn_g[l], casts, seq=seq)
        if l == 0:
            (w_gate_up_l,), (w_down_l,) = converted[:2]
            w_out_l, w_branch_a_l, w_branch_b_l = converted[2][:3]
        outs = [_dilated_attention(qk, vb, bsz=bsz, seq=seq, group=g) for g in range(N_GROUPS)]
        o_b, next_gate_up = _stick_breaking(vb, nxt(w_gate_up), bsz=bsz, seq=seq)
        h, u2 = _merge(h, [o for o, _ in outs], [s for _, s in outs], o_b, gates, mod3, 2,
                       norm2_g[l].reshape(1, d_model), 3, 4,
                       w_branch_a_l, w_branch_b_l, w_out_l, seq=seq)
        h, next_rest = _ffn(h, u2, mod3, 5, w_gate_up_l, w_down_l,
                            nxt(w_down, w_out, w_branch_a, w_branch_b), seq=seq)
        if has_next:
            w_in_l, (w_gate_up_l,) = converted[2][-1], next_gate_up
            w_down_l, w_out_l, w_branch_a_l, w_branch_b_l = next_rest
    return h.reshape(bsz, seq, d_model)
```

```python
import functools

import jax
import jax.numpy as jnp
from jax import lax
from jax.experimental import pallas as pl
from jax.experimental.pallas import tpu as pltpu

F32 = jnp.float32
BF16 = jnp.bfloat16

HEAD_DIM = 128
HEADS_PER_GROUP = 4
GROUP_W = HEADS_PER_GROUP * HEAD_DIM
DIL_PATTERNS = ((128, 1), (512, 4), (2048, 16))
N_GROUPS = len(DIL_PATTERNS)
WIN_BLOCK = 128
PERM_BLOCK = 256
BF16_SUBLANES = 16
ROPE_THETA = 10000.0
EPS = 1e-6
QK_SCALE = HEAD_DIM ** -0.5

VMEM_LIMIT_BYTES = 56 * 1024 * 1024


def _params(*semantics):
    return pltpu.CompilerParams(dimension_semantics=semantics,
                                vmem_limit_bytes=VMEM_LIMIT_BYTES)


def _dot(a, b):
    return jnp.dot(a, b, preferred_element_type=F32)


def _dot_nt(a, b):
    return lax.dot_general(a, b, (((1,), (1,)), ((), ())), preferred_element_type=F32)


def _mod_kernel(c_ref, w_ref, b_ref, o_ref):
    c = c_ref[...]
    c_act = (c * jax.nn.sigmoid(c)).astype(BF16)
    o_ref[...] = _dot(c_act, w_ref[...].astype(BF16)) + b_ref[...]


def _modulation(c, w_ada, b_ada, tn=512):
    depth, d_model, n = w_ada.shape
    bsz = c.shape[0]
    return pl.pallas_call(
        _mod_kernel,
        grid=(depth, n // tn),
        in_specs=[
            pl.BlockSpec((bsz, d_model), lambda l, j: (0, 0)),
            pl.BlockSpec((None, d_model, tn), lambda l, j: (l, 0, j)),
            pl.BlockSpec((None, 1, tn), lambda l, j: (l, 0, j)),
        ],
        out_specs=pl.BlockSpec((None, bsz, tn), lambda l, j: (l, 0, j)),
        out_shape=jax.ShapeDtypeStruct((depth, bsz, n), F32),
        compiler_params=_params("parallel", "parallel"),
        name="modulation",
    )(c, w_ada, b_ada.reshape(depth, 1, n))


def _modulated_norm(h, g, scale, shift):
    ms = jnp.mean(h * h, axis=-1, keepdims=True)
    return (h * lax.rsqrt(ms + EPS)) * (g * (1.0 + scale)) + shift


def _norm_kernel(h_ref, g_ref, scale_ref, shift_ref, o_ref):
    o_ref[...] = _modulated_norm(h_ref[...], g_ref[...], scale_ref[...],
                                 shift_ref[...]).astype(BF16)


def _norm_modulate(h, g, mod3, shift_idx, scale_idx, seq, tm=512):
    tokens, d_model = h.shape
    blocks_per_seq = seq // tm
    return pl.pallas_call(
        _norm_kernel,
        grid=(tokens // tm,),
        in_specs=[
            pl.BlockSpec((tm, d_model), lambda i: (i, 0)),
            pl.BlockSpec((1, d_model), lambda i: (0, 0)),
            pl.BlockSpec((None, 1, d_model), lambda i: (i // blocks_per_seq, 0, scale_idx)),
            pl.BlockSpec((None, 1, d_model), lambda i: (i // blocks_per_seq, 0, shift_idx)),
        ],
        out_specs=pl.BlockSpec((tm, d_model), lambda i: (i, 0)),
        out_shape=jax.ShapeDtypeStruct((tokens, d_model), BF16),
        compiler_params=_params("parallel"),
        name="norm_modulate",
    )(h, g, mod3, mod3)


def _rope_epilogue(acc, cos_ref, sin_ref, gain_ref, o_ref):
    cos = cos_ref[...]
    sin = sin_ref[...]
    for hh in range(acc.shape[1] // HEAD_DIM):
        cs = slice(hh * HEAD_DIM, (hh + 1) * HEAD_DIM)
        x = acc[:, cs]
        ms = jnp.mean(x * x, axis=-1, keepdims=True)
        y = (x * lax.rsqrt(ms + EPS)) * gain_ref[:, cs]
        o_ref[:, cs] = (y * cos + pltpu.roll(y, HEAD_DIM // 2, 1) * sin).astype(BF16)


def _scaled_epilogue(acc, mult_ref, o_ref):
    o_ref[...] = (acc * mult_ref[...]).astype(BF16)


def _sigmoid_epilogue(acc, o_ref):
    o_ref[...] = jax.nn.sigmoid(acc).astype(BF16)


def _cast_specs(casts, n_steps, step_of):
    in_specs, out_specs, out_shapes = [], [], []
    for w, layer in casts:
        _, n_rows, n_cols = w.shape
        rows = next(r for r in range(BF16_SUBLANES, n_rows + 1, BF16_SUBLANES)
                    if n_rows % r == 0 and n_rows // r <= n_steps)
        chunk = lambda *ids, last=n_rows // rows - 1: jnp.minimum(step_of(*ids), last)
        in_specs.append(pl.BlockSpec((None, rows, n_cols),
                                     lambda *ids, layer=layer, chunk=chunk: (layer, chunk(*ids), 0)))
        out_specs.append(pl.BlockSpec((rows, n_cols), lambda *ids, chunk=chunk: (chunk(*ids), 0)))
        out_shapes.append(jax.ShapeDtypeStruct((n_rows, n_cols), BF16))
    return in_specs, out_specs, out_shapes


def _run_casts(src_refs, dst_refs):
    for src, dst in zip(src_refs, dst_refs):
        dst[...] = src[...].astype(BF16)


def _proj_kernel(u_ref, w_ref, *refs, epilogue, n_cast):
    n_extra = len(refs) - 2 * n_cast - 2
    extra_refs, cast_in = refs[:n_extra], refs[n_extra:n_extra + n_cast]
    o_ref, cast_out, acc_ref = refs[n_extra + n_cast], refs[n_extra + n_cast + 1:-1], refs[-1]

    @pl.when(pl.program_id(0) == 0)
    def _():
        acc_ref[...] = jnp.zeros(acc_ref.shape, F32)

    epilogue(acc_ref[...], *extra_refs, o_ref)
    acc_ref[...] = _dot(u_ref[...], w_ref[...])
    _run_casts(cast_in, cast_out)


def _projection(epilogue, name, u, w, first_col, n_cols, tile_w, extra_specs, extra_args,
                casts=(), tm=1024):
    tokens, d_model = u.shape
    n_tiles = n_cols // tile_w
    first_tile = first_col // tile_w
    n_pairs = (tokens // tm) * n_tiles
    mm = lambda t: jnp.minimum(t, n_pairs - 1)
    ep = lambda t: jnp.maximum(t - 1, 0)
    lagged = lambda spec: pl.BlockSpec(
        spec.block_shape, lambda t, f=spec.index_map: f(ep(t) // n_tiles, ep(t) % n_tiles))
    cast_in, cast_out, cast_shapes = _cast_specs(casts, n_pairs + 1, lambda t: t)
    out, *converted = pl.pallas_call(
        functools.partial(_proj_kernel, epilogue=epilogue, n_cast=len(casts)),
        grid=(n_pairs + 1,),
        in_specs=[
            pl.BlockSpec((tm, d_model), lambda t: (mm(t) // n_tiles, 0)),
            pl.BlockSpec((d_model, tile_w), lambda t: (0, first_tile + mm(t) % n_tiles)),
        ] + [lagged(spec) for spec in extra_specs] + cast_in,
        out_specs=[lagged(pl.BlockSpec((tm, tile_w), lambda i, j: (i, j)))] + cast_out,
        out_shape=[jax.ShapeDtypeStruct((tokens, n_cols), BF16)] + cast_shapes,
        scratch_shapes=[pltpu.VMEM((tm, tile_w), F32)],
        compiler_params=_params("arbitrary"),
        name=name,
    )(u, w, *extra_args, *[w_f32 for w_f32, _ in casts])
    return out, converted


def _input_projections(u, w_in, cos_full, sin_full, qn_g, kn_g, casts, *, seq, tm=1024):
    d_model = u.shape[1]
    blocks_per_seq = seq // tm
    a_w = N_GROUPS * GROUP_W
    col_spec = pl.BlockSpec((1, a_w), lambda i, j: (0, j))
    rope_spec = pl.BlockSpec((tm, HEAD_DIM), lambda i, j: (i % blocks_per_seq, 0))
    gains = jnp.concatenate([jnp.tile(qn_g * QK_SCALE, a_w // HEAD_DIM),
                             jnp.tile(kn_g, a_w // HEAD_DIM)]).reshape(1, 2 * a_w)
    qk, conv_qk = _projection(_rope_epilogue, "projection_qk", u, w_in, 0, 2 * a_w, a_w,
                              [rope_spec, rope_spec, col_spec], [cos_full, sin_full, gains],
                              casts[0], tm=tm)
    vb_w = a_w + 3 * GROUP_W
    mult = jnp.ones((1, vb_w), F32).at[:, a_w:a_w + GROUP_W].set(QK_SCALE)
    vb, conv_vb = _projection(_scaled_epilogue, "projection_vb", u, w_in, 2 * a_w, vb_w, a_w,
                              [col_spec], [mult], casts[1], tm=tm)
    gates, conv_gates = _projection(_sigmoid_epilogue, "projection_gates", u, w_in,
                                    2 * a_w + vb_w, 2 * d_model, d_model, [], [], casts[2],
                                    tm=tm)
    return qk, vb, gates, (conv_qk, conv_vb, conv_gates)


def _residue_permutation(dil, transpose):
    n = PERM_BLOCK // dil
    a = lax.broadcasted_iota(jnp.int32, (PERM_BLOCK, PERM_BLOCK), 1 if transpose else 0)
    b = lax.broadcasted_iota(jnp.int32, (PERM_BLOCK, PERM_BLOCK), 0 if transpose else 1)
    return (b == (a % n) * dil + a // n).astype(BF16)


def _dilated_kernel(q_ref, k_ref, v_ref, o_ref, lse_ref, qs_ref, ks_ref, vs_ref, os_ref, ls_ref,
                    *, dil, rows_per_batch):
    seq = q_ref.shape[0]
    sub_len = seq // dil
    n = PERM_BLOCK // dil
    pad = WIN_BLOCK

    zeros = jnp.zeros((pad, GROUP_W), BF16)
    ks_ref[0:pad, :] = zeros
    vs_ref[0:pad, :] = zeros
    if dil == 1:
        ks_ref[pad:, :] = k_ref[...]
        vs_ref[pad:, :] = v_ref[...]
        q_src = q_ref
    else:
        perm = _residue_permutation(dil, transpose=False)
        for src, dst, off in ((q_ref, qs_ref, 0), (k_ref, ks_ref, pad), (v_ref, vs_ref, pad)):
            for nb in range(seq // PERM_BLOCK):
                y = _dot(perm, src[nb * PERM_BLOCK:(nb + 1) * PERM_BLOCK, :]).astype(BF16)
                for r in range(dil):
                    d0 = off + r * sub_len + nb * n
                    dst[d0:d0 + n, :] = y[r * n:(r + 1) * n, :]
        q_src = qs_ref
    o_dst, l_dst = (o_ref, lse_ref) if dil == 1 else (os_ref, ls_ref)

    qi_ = lax.broadcasted_iota(jnp.int32, (WIN_BLOCK, 2 * WIN_BLOCK), 0)
    kj_ = lax.broadcasted_iota(jnp.int32, (WIN_BLOCK, 2 * WIN_BLOCK), 1)
    band = (kj_ >= qi_) & (kj_ <= qi_ + WIN_BLOCK)
    band_first = band & (kj_ >= WIN_BLOCK)
    lane = lax.broadcasted_iota(jnp.int32, (WIN_BLOCK, HEAD_DIM), 1)
    n_qb = rows_per_batch // WIN_BLOCK
    for b0 in range(0, seq, rows_per_batch):
        units = [(b0 + qb * WIN_BLOCK, hh) for qb in range(n_qb) for hh in range(HEADS_PER_GROUP)]
        scores = []
        for r0, hh in units:
            cs = slice(hh * HEAD_DIM, (hh + 1) * HEAD_DIM)
            s = _dot_nt(q_src[r0:r0 + WIN_BLOCK, cs], ks_ref[r0:r0 + 2 * WIN_BLOCK, cs])
            mask = band_first if r0 % sub_len == 0 else band
            scores.append(jnp.where(mask, s, -jnp.inf))
        s_all = jnp.concatenate(scores, axis=0)
        m = jnp.max(s_all, axis=-1, keepdims=True)
        p = jnp.exp(s_all - m)
        den = jnp.sum(p, axis=-1, keepdims=True)
        p = p.astype(BF16)
        inv = 1.0 / den
        lse = m + jnp.log(den)
        for ui, (r0, hh) in enumerate(units):
            cs = slice(hh * HEAD_DIM, (hh + 1) * HEAD_DIM)
            us = slice(ui * WIN_BLOCK, (ui + 1) * WIN_BLOCK)
            pv = _dot(p[us], vs_ref[r0:r0 + 2 * WIN_BLOCK, cs])
            o_dst[r0:r0 + WIN_BLOCK, cs] = (pv * inv[us]).astype(BF16)
            if hh == 0:
                lse_tile = jnp.zeros((WIN_BLOCK, HEAD_DIM), F32)
            lse_tile = jnp.where(lane == hh, lse[us], lse_tile)
            if hh == HEADS_PER_GROUP - 1:
                l_dst[r0:r0 + WIN_BLOCK, :] = lse_tile

    if dil > 1:
        perm_t = _residue_permutation(dil, transpose=True)
        for nb in range(seq // PERM_BLOCK):
            rows = slice(nb * PERM_BLOCK, (nb + 1) * PERM_BLOCK)
            pieces = [slice(r * sub_len + nb * n, r * sub_len + (nb + 1) * n) for r in range(dil)]
            o_blk = jnp.concatenate([os_ref[pc, :] for pc in pieces], axis=0)
            o_ref[rows, :] = _dot(perm_t, o_blk).astype(BF16)
            for r, pc in enumerate(pieces):
                lse_ref[pl.ds(nb * PERM_BLOCK + r, n, stride=dil), :] = ls_ref[pc, :]


def _dilated_attention(qk, vb, *, bsz, seq, group, rows_per_batch=512):
    tokens = qk.shape[0]
    dil = DIL_PATTERNS[group][1]
    kern = functools.partial(_dilated_kernel, dil=dil, rows_per_batch=rows_per_batch)
    scratch = [
        pltpu.VMEM((seq, GROUP_W), BF16),
        pltpu.VMEM((seq + WIN_BLOCK, GROUP_W), BF16),
        pltpu.VMEM((seq + WIN_BLOCK, GROUP_W), BF16),
        pltpu.VMEM((seq, GROUP_W), BF16),
        pltpu.VMEM((seq, HEAD_DIM), F32),
    ]
    return pl.pallas_call(
        kern,
        grid=(bsz,),
        in_specs=[
            pl.BlockSpec((seq, GROUP_W), lambda b: (b, group)),
            pl.BlockSpec((seq, GROUP_W), lambda b: (b, N_GROUPS + group)),
            pl.BlockSpec((seq, GROUP_W), lambda b: (b, group)),
        ],
        out_specs=[
            pl.BlockSpec((seq, GROUP_W), lambda b: (b, 0)),
            pl.BlockSpec((seq, HEAD_DIM), lambda b: (b, 0)),
        ],
        out_shape=[
            jax.ShapeDtypeStruct((tokens, GROUP_W), BF16),
            jax.ShapeDtypeStruct((tokens, HEAD_DIM), F32),
        ],
        scratch_shapes=scratch,
        compiler_params=_params("parallel"),
        name=f"dilated_attention_g{group}",
    )(qk, qk, vb)


def _sb_kernel(q_ref, k_ref, v_ref, *refs, tq, tk, n_cast):
    cast_in, o_ref, cast_out = refs[:n_cast], refs[n_cast], refs[n_cast + 1:]
    _run_casts(cast_in, cast_out)
    qi = pl.program_id(1)
    heads = q_ref.shape[1] // HEAD_DIM
    n_seg = tq // tk
    head_cols = [slice(hh * HEAD_DIM, (hh + 1) * HEAD_DIM) for hh in range(heads)]
    r = lax.broadcasted_iota(jnp.int32, (tk, tk), 0)
    c = lax.broadcasted_iota(jnp.int32, (tk, tk), 1)
    suffix = (r > c).astype(BF16)
    q_row = lax.broadcasted_iota(jnp.int32, (heads * tk, tk), 0) % tk
    k_col = lax.broadcasted_iota(jnp.int32, (heads * tk, tk), 1)
    strictly_before = k_col < q_row

    def block(kblk, r0, r1, on_diagonal):
        rows = pl.ds(pl.multiple_of(kblk * tk, tk), tk)
        n = r1 - r0
        z = jnp.concatenate([_dot_nt(q_ref[r0:r1, cs], k_ref[rows, cs]) for cs in head_cols],
                            axis=0)
        sp = jnp.log(1.0 + jnp.exp(-jnp.abs(z))) + jnp.maximum(z, 0.0)
        log_beta = z - sp
        if on_diagonal:
            sp = jnp.where(strictly_before, sp, 0.0)
        local = _dot(sp.astype(BF16), suffix)
        a = jnp.exp(log_beta - local)
        if on_diagonal:
            a = jnp.where(strictly_before, a, 0.0)
        a = a.astype(BF16)
        pv = jnp.concatenate([_dot(a[hh * n:(hh + 1) * n], v_ref[rows, cs])
                              for hh, cs in enumerate(head_cols)], axis=0)
        return pv, local[:, 0:1] + sp[:, 0:1]

    def absorb(acc, later, r0, r1, pv, total):
        if (r0, r1) == (0, tq):
            return acc + jnp.exp(-later) * pv, later + total
        n = r1 - r0
        acc_parts, later_parts = [], []
        for hh in range(heads):
            lo, hi = hh * tq + r0, hh * tq + r1
            piece = slice(hh * n, (hh + 1) * n)
            for parts, old, new in (
                    (acc_parts, acc, acc[lo:hi] + jnp.exp(-later[lo:hi]) * pv[piece]),
                    (later_parts, later, later[lo:hi] + total[piece])):
                parts += [old[hh * tq:lo]] if r0 else []
                parts.append(new)
                parts += [old[hi:(hh + 1) * tq]] if r1 < tq else []
        return jnp.concatenate(acc_parts, axis=0), jnp.concatenate(later_parts, axis=0)

    acc = jnp.zeros((heads * tq, HEAD_DIM), F32)
    later = jnp.zeros((heads * tq, 1), F32)
    for d in reversed(range(n_seg)):
        r0, r1 = d * tk, (d + 1) * tk
        acc, later = absorb(acc, later, r0, r1, *block(qi * n_seg + d, r0, r1, True))
        if r1 < tq:
            acc, later = absorb(acc, later, r1, tq, *block(qi * n_seg + d, r1, tq, False))

    def below_diagonal(t, carry):
        acc, later = carry
        for d in range(n_seg):
            acc, later = absorb(acc, later, 0, tq,
                                *block((qi - t) * n_seg - 1 - d, 0, tq, False))
        return acc, later

    acc, later = lax.fori_loop(0, qi, below_diagonal, (acc, later))
    for hh, cs in enumerate(head_cols):
        o_ref[:, cs] = acc[hh * tq:(hh + 1) * tq].astype(BF16)


def _stick_breaking(vb, casts=(), *, bsz, seq, tq=512, tk=256):
    tokens = vb.shape[0]
    nq = seq // tq
    cast_in, cast_out, cast_shapes = _cast_specs(casts, bsz * nq, lambda b, i: b * nq + i)
    kern = functools.partial(_sb_kernel, tq=tq, tk=tk, n_cast=len(casts))
    out, *converted = pl.pallas_call(
        kern,
        grid=(bsz, nq),
        in_specs=[
            pl.BlockSpec((tq, GROUP_W), lambda b, i: (b * nq + i, N_GROUPS)),
            pl.BlockSpec((seq, GROUP_W), lambda b, i: (b, N_GROUPS + 1)),
            pl.BlockSpec((seq, GROUP_W), lambda b, i: (b, N_GROUPS + 2)),
        ] + cast_in,
        out_specs=[pl.BlockSpec((tq, GROUP_W), lambda b, i: (b * nq + i, 0))] + cast_out,
        out_shape=[jax.ShapeDtypeStruct((tokens, GROUP_W), BF16)] + cast_shapes,
        compiler_params=_params("arbitrary", "arbitrary"),
        name="stick_breaking",
    )(vb, vb, vb, *[w_f32 for w_f32, _ in casts])
    return out, converted


def _merge_kernel(h_ref, o0_ref, o1_ref, o2_ref, l0_ref, l1_ref, l2_ref, ob_ref,
                  sga_ref, sgb_ref, gate_ref, g2_ref, scale2_ref, shift2_ref,
                  wa_ref, wb_ref, wo_ref, out_ref, u2_ref, oa_ref):
    l0, l1, l2 = l0_ref[...], l1_ref[...], l2_ref[...]
    m = jnp.maximum(jnp.maximum(l0, l1), l2)
    e0 = jnp.exp(l0 - m)
    e1 = jnp.exp(l1 - m)
    e2 = jnp.exp(l2 - m)
    inv = 1.0 / (e0 + e1 + e2)
    w0, w1, w2 = e0 * inv, e1 * inv, e2 * inv
    for hh in range(HEADS_PER_GROUP):
        cs = slice(hh * HEAD_DIM, (hh + 1) * HEAD_DIM)
        oa = (w0[:, hh:hh + 1] * o0_ref[:, cs].astype(F32)
              + w1[:, hh:hh + 1] * o1_ref[:, cs].astype(F32)
              + w2[:, hh:hh + 1] * o2_ref[:, cs].astype(F32))
        oa_ref[:, cs] = oa.astype(BF16)
    y_a = _dot(oa_ref[...], wa_ref[...])
    y_b = _dot(ob_ref[...], wb_ref[...])
    merged = sga_ref[...].astype(F32) * y_a + sgb_ref[...].astype(F32) * y_b
    h_new = h_ref[...] + gate_ref[...] * _dot(merged.astype(BF16), wo_ref[...])
    out_ref[...] = h_new
    u2_ref[...] = _modulated_norm(h_new, g2_ref[...], scale2_ref[...],
                                  shift2_ref[...]).astype(BF16)


def _merge(h, o_groups, lse_groups, o_b, gates, mod3, gate_idx, g2, shift2_idx, scale2_idx,
           wa, wb, wo, *, seq, tm=512):
    tokens, d_model = h.shape
    blocks_per_seq = seq // tm
    row = lambda i: (i, 0)
    resident = lambda w: pl.BlockSpec(w.shape, lambda i: (0, 0), pipeline_mode=pl.Buffered(1))
    mod_spec = lambda idx: pl.BlockSpec((None, 1, d_model),
                                        lambda i: (i // blocks_per_seq, 0, idx))
    return pl.pallas_call(
        _merge_kernel,
        grid=(tokens // tm,),
        in_specs=[
            pl.BlockSpec((tm, d_model), row),
            pl.BlockSpec((tm, GROUP_W), row),
            pl.BlockSpec((tm, GROUP_W), row),
            pl.BlockSpec((tm, GROUP_W), row),
            pl.BlockSpec((tm, HEAD_DIM), row),
            pl.BlockSpec((tm, HEAD_DIM), row),
            pl.BlockSpec((tm, HEAD_DIM), row),
            pl.BlockSpec((tm, GROUP_W), row),
            pl.BlockSpec((tm, d_model), lambda i: (i, 0)),
            pl.BlockSpec((tm, d_model), lambda i: (i, 1)),
            mod_spec(gate_idx),
            pl.BlockSpec((1, d_model), lambda i: (0, 0)),
            mod_spec(scale2_idx),
            mod_spec(shift2_idx),
            resident(wa),
            resident(wb),
            resident(wo),
        ],
        out_specs=[pl.BlockSpec((tm, d_model), row), pl.BlockSpec((tm, d_model), row)],
        out_shape=[jax.ShapeDtypeStruct((tokens, d_model), F32),
                   jax.ShapeDtypeStruct((tokens, d_model), BF16)],
        scratch_shapes=[pltpu.VMEM((tm, GROUP_W), BF16)],
        compiler_params=_params("parallel"),
        name="merge",
    )(h, *o_groups, *lse_groups, o_b, gates, gates, mod3, g2, mod3, mod3, wa, wb, wo)


def _ffn_kernel(u_ref, hres_ref, gate_ref, wg_ref, wu_ref, wd_ref, *refs, nf, tf, n_cast):
    cast_in, out_ref, cast_out, act_ref = refs[:n_cast], refs[n_cast], refs[n_cast + 1:-1], refs[-1]
    s = pl.program_id(1)

    @pl.when(s < nf)
    def _():
        u = u_ref[...]
        gate_act = _dot(u, wg_ref[...])
        up = _dot(u, wu_ref[...])
        cols = pl.ds(pl.multiple_of(s * tf, tf), tf)
        act_ref[:, cols] = (gate_act * jax.nn.sigmoid(gate_act) * up).astype(BF16)
        _run_casts(cast_in, cast_out)

    @pl.when(s >= nf)
    def _():
        out_ref[...] = hres_ref[...] + gate_ref[...] * _dot(act_ref[...], wd_ref[...])
        _run_casts(cast_in, cast_out)


def _ffn(h, u2, mod3, gate_idx, w_gate_up, w_down, casts=(), *, seq, tm=1024, tf=512, tn=512):
    tokens, d_model = h.shape
    d_ff = w_down.shape[0]
    nf = d_ff // tf
    nd = d_model // tn
    steps = nf + nd
    blocks_per_seq = seq // tm
    up_step = lambda s: jnp.minimum(s, nf - 1)
    down_step = lambda s: jnp.clip(s - nf, 0, nd - 1)
    cast_in, cast_out, cast_shapes = _cast_specs(casts, (tokens // tm) * steps,
                                                 lambda i, s: i * steps + s)
    kern = functools.partial(_ffn_kernel, nf=nf, tf=tf, n_cast=len(casts))
    out, *converted = pl.pallas_call(
        kern,
        grid=(tokens // tm, steps),
        in_specs=[
            pl.BlockSpec((tm, d_model), lambda i, s: (i, 0)),
            pl.BlockSpec((tm, tn), lambda i, s: (i, down_step(s))),
            pl.BlockSpec((None, 1, tn),
                         lambda i, s: (i // blocks_per_seq, 0, gate_idx * nd + down_step(s))),
            pl.BlockSpec((d_model, tf), lambda i, s: (0, up_step(s))),
            pl.BlockSpec((d_model, tf), lambda i, s: (0, nf + up_step(s))),
            pl.BlockSpec((d_ff, tn), lambda i, s: (0, down_step(s))),
        ] + cast_in,
        out_specs=[pl.BlockSpec((tm, tn), lambda i, s: (i, down_step(s)))] + cast_out,
        out_shape=[jax.ShapeDtypeStruct((tokens, d_model), F32)] + cast_shapes,
        scratch_shapes=[pltpu.VMEM((tm, d_ff), BF16)],
        compiler_params=_params("arbitrary", "arbitrary"),
        name="ffn",
    )(u2, h, mod3, w_gate_up, w_gate_up, w_down, *[w_f32 for w_f32, _ in casts])
    return out, converted


def _rope_tables(seq):
    inv = jnp.power(ROPE_THETA, -jnp.arange(0, HEAD_DIM, 2, dtype=F32) / HEAD_DIM)
    ang = jnp.arange(seq, dtype=F32)[:, None] * inv[None, :]
    cos, sin = jnp.cos(ang), jnp.sin(ang)
    return jnp.concatenate([cos, cos], axis=-1), jnp.concatenate([-sin, sin], axis=-1)


def kernel(x, c, w_ada, b_ada, norm1_g, norm2_g, w_in, qn_g, kn_g, w_branch_a, w_branch_b,
           w_out, w_gate_up, w_down):
    bsz, seq, d_model = x.shape
    depth = w_ada.shape[0]
    tokens = bsz * seq
    cos_full, sin_full = _rope_tables(seq)
    mod = _modulation(c, w_ada, b_ada)
    w_in_l = w_in[0].astype(BF16)
    h = x.reshape(tokens, d_model)
    for l in range(depth):
        mod3 = mod[l].reshape(bsz, 1, 6 * d_model)
        u = _norm_modulate(h, norm1_g[l].reshape(1, d_model), mod3, 0, 1, seq)
        has_next = l + 1 < depth
        nxt = lambda *ws: [(w, l + 1) for w in ws] if has_next else []
        first = lambda *ws: [(w, 0) for w in ws] if l == 0 else []
        casts = (first(w_gate_up), first(w_down),
                 first(w_out, w_branch_a, w_branch_b) + nxt(w_in))
        qk, vb, gates, converted = _input_projections(u, w_in_l, cos_full, sin_full, qn_g[l],
                                                      kn_g[l], casts, seq=seq)
        if l == 0:
            (w_gate_up_l,), (w_down_l,) = converted[:2]
            w_out_l, w_branch_a_l, w_branch_b_l = converted[2][:3]
        outs = [_dilated_attention(qk, vb, bsz=bsz, seq=seq, group=g) for g in range(N_GROUPS)]
        o_b, next_gate_up = _stick_breaking(vb, nxt(w_gate_up), bsz=bsz, seq=seq)
        h, u2 = _merge(h, [o for o, _ in outs], [s for _, s in outs], o_b, gates, mod3, 2,
                       norm2_g[l].reshape(1, d_model), 3, 4,
                       w_branch_a_l, w_branch_b_l, w_out_l, seq=seq)
        h, next_rest = _ffn(h, u2, mod3, 5, w_gate_up_l, w_down_l,
                            nxt(w_down, w_out, w_branch_a, w_branch_b), seq=seq)
        if has_next:
            w_in_l, (w_gate_up_l,) = converted[2][-1], next_gate_up
            w_down_l, w_out_l, w_branch_a_l, w_branch_b_l = next_rest
    return h.reshape(bsz, seq, d_model)
```

```python
import functools

import jax
import jax.numpy as jnp
from jax import lax
from jax.experimental import pallas as pl
from jax.experimental.pallas import tpu as pltpu

F32 = jnp.float32
BF16 = jnp.bfloat16

HEAD_DIM = 128
HEADS_PER_GROUP = 4
GROUP_W = HEADS_PER_GROUP * HEAD_DIM
DIL_PATTERNS = ((128, 1), (512, 4), (2048, 16))
N_GROUPS = len(DIL_PATTERNS)
WIN_BLOCK = 128
PERM_BLOCK = 256
BF16_SUBLANES = 16
MERGE_PARTS = 4
ROPE_THETA = 10000.0
EPS = 1e-6
QK_SCALE = HEAD_DIM ** -0.5

VMEM_LIMIT_BYTES = 56 * 1024 * 1024


def _params(*semantics):
    return pltpu.CompilerParams(dimension_semantics=semantics,
                                vmem_limit_bytes=VMEM_LIMIT_BYTES)


def _dot(a, b):
    return jnp.dot(a, b, preferred_element_type=F32)


def _dot_nt(a, b):
    return lax.dot_general(a, b, (((1,), (1,)), ((), ())), preferred_element_type=F32)


def _mod_kernel(c_ref, w_ref, b_ref, o_ref):
    c = c_ref[...]
    c_act = (c * jax.nn.sigmoid(c)).astype(BF16)
    o_ref[...] = _dot(c_act, w_ref[...].astype(BF16)) + b_ref[...]


def _modulation(c, w_ada, b_ada, tn=1024):
    depth, d_model, n = w_ada.shape
    bsz = c.shape[0]
    return pl.pallas_call(
        _mod_kernel,
        grid=(depth, n // tn),
        in_specs=[
            pl.BlockSpec((bsz, d_model), lambda l, j: (0, 0)),
            pl.BlockSpec((None, d_model, tn), lambda l, j: (l, 0, j)),
            pl.BlockSpec((None, 1, tn), lambda l, j: (l, 0, j)),
        ],
        out_specs=pl.BlockSpec((None, bsz, tn), lambda l, j: (l, 0, j)),
        out_shape=jax.ShapeDtypeStruct((depth, bsz, n), F32),
        compiler_params=_params("parallel", "parallel"),
        name="modulation",
    )(c, w_ada, b_ada.reshape(depth, 1, n))


def _modulated_norm(h, g, scale, shift):
    ms = jnp.mean(h * h, axis=-1, keepdims=True)
    return (h * lax.rsqrt(ms + EPS)) * (g * (1.0 + scale)) + shift


def _norm_kernel(h_ref, g_ref, scale_ref, shift_ref, o_ref):
    o_ref[...] = _modulated_norm(h_ref[...], g_ref[...], scale_ref[...],
                                 shift_ref[...]).astype(BF16)


def _norm_modulate(h, g, mod3, shift_idx, scale_idx, seq, tm=1024):
    tokens, d_model = h.shape
    blocks_per_seq = seq // tm
    return pl.pallas_call(
        _norm_kernel,
        grid=(tokens // tm,),
        in_specs=[
            pl.BlockSpec((tm, d_model), lambda i: (i, 0)),
            pl.BlockSpec((1, d_model), lambda i: (0, 0)),
            pl.BlockSpec((None, 1, d_model), lambda i: (i // blocks_per_seq, 0, scale_idx)),
            pl.BlockSpec((None, 1, d_model), lambda i: (i // blocks_per_seq, 0, shift_idx)),
        ],
        out_specs=pl.BlockSpec((tm, d_model), lambda i: (i, 0)),
        out_shape=jax.ShapeDtypeStruct((tokens, d_model), BF16),
        compiler_params=_params("parallel"),
        name="norm_modulate",
    )(h, g, mod3, mod3)


def _rope_epilogue(acc, cos_ref, sin_ref, gain_ref, o_ref):
    cos = cos_ref[...]
    sin = sin_ref[...]
    for hh in range(acc.shape[1] // HEAD_DIM):
        cs = slice(hh * HEAD_DIM, (hh + 1) * HEAD_DIM)
        x = acc[:, cs]
        ms = jnp.mean(x * x, axis=-1, keepdims=True)
        y = (x * lax.rsqrt(ms + EPS)) * gain_ref[:, cs]
        o_ref[:, cs] = (y * cos + pltpu.roll(y, HEAD_DIM // 2, 1) * sin).astype(BF16)


def _scaled_epilogue(acc, mult_ref, o_ref):
    o_ref[...] = (acc * mult_ref[...]).astype(BF16)


def _sigmoid_epilogue(acc, o_ref):
    o_ref[...] = jax.nn.sigmoid(acc).astype(BF16)


def _cast_specs(casts, n_steps, step_of):
    in_specs, out_specs, out_shapes = [], [], []
    for w, layer in casts:
        _, n_rows, n_cols = w.shape
        rows = next(r for r in range(BF16_SUBLANES, n_rows + 1, BF16_SUBLANES)
                    if n_rows % r == 0 and n_rows // r <= n_steps)
        chunk = lambda *ids, last=n_rows // rows - 1: jnp.minimum(step_of(*ids), last)
        in_specs.append(pl.BlockSpec((None, rows, n_cols),
                                     lambda *ids, layer=layer, chunk=chunk: (layer, chunk(*ids), 0)))
        out_specs.append(pl.BlockSpec((rows, n_cols), lambda *ids, chunk=chunk: (chunk(*ids), 0)))
        out_shapes.append(jax.ShapeDtypeStruct((n_rows, n_cols), BF16))
    return in_specs, out_specs, out_shapes


def _run_casts(src_refs, dst_refs):
    for src, dst in zip(src_refs, dst_refs):
        dst[...] = src[...].astype(BF16)


def _proj_kernel(u_ref, w_ref, *refs, epilogue, n_cast):
    n_extra = len(refs) - 2 * n_cast - 2
    extra_refs, cast_in = refs[:n_extra], refs[n_extra:n_extra + n_cast]
    o_ref, cast_out, acc_ref = refs[n_extra + n_cast], refs[n_extra + n_cast + 1:-1], refs[-1]

    @pl.when(pl.program_id(0) == 0)
    def _():
        acc_ref[...] = jnp.zeros(acc_ref.shape, F32)

    epilogue(acc_ref[...], *extra_refs, o_ref)
    acc_ref[...] = _dot(u_ref[...], w_ref[...])
    _run_casts(cast_in, cast_out)


def _projection(epilogue, name, u, w, first_col, n_cols, tile_w, extra_specs, extra_args,
                casts=(), tm=1024):
    tokens, d_model = u.shape
    n_tiles = n_cols // tile_w
    first_tile = first_col // tile_w
    n_pairs = (tokens // tm) * n_tiles
    mm = lambda t: jnp.minimum(t, n_pairs - 1)
    ep = lambda t: jnp.maximum(t - 1, 0)
    lagged = lambda spec: pl.BlockSpec(
        spec.block_shape, lambda t, f=spec.index_map: f(ep(t) // n_tiles, ep(t) % n_tiles))
    cast_in, cast_out, cast_shapes = _cast_specs(casts, n_pairs + 1, lambda t: t)
    out, *converted = pl.pallas_call(
        functools.partial(_proj_kernel, epilogue=epilogue, n_cast=len(casts)),
        grid=(n_pairs + 1,),
        in_specs=[
            pl.BlockSpec((tm, d_model), lambda t: (mm(t) // n_tiles, 0)),
            pl.BlockSpec((d_model, tile_w), lambda t: (0, first_tile + mm(t) % n_tiles)),
        ] + [lagged(spec) for spec in extra_specs] + cast_in,
        out_specs=[lagged(pl.BlockSpec((tm, tile_w), lambda i, j: (i, j)))] + cast_out,
        out_shape=[jax.ShapeDtypeStruct((tokens, n_cols), BF16)] + cast_shapes,
        scratch_shapes=[pltpu.VMEM((tm, tile_w), F32)],
        compiler_params=_params("arbitrary"),
        name=name,
    )(u, w, *extra_args, *[w_f32 for w_f32, _ in casts])
    return out, converted


def _input_projections(u, w_in, cos_full, sin_full, qn_g, kn_g, casts, *, seq, tm=1024):
    d_model = u.shape[1]
    blocks_per_seq = seq // tm
    a_w = N_GROUPS * GROUP_W
    col_spec = pl.BlockSpec((1, a_w), lambda i, j: (0, j))
    rope_spec = pl.BlockSpec((tm, HEAD_DIM), lambda i, j: (i % blocks_per_seq, 0))
    gains = jnp.concatenate([jnp.tile(qn_g * QK_SCALE, a_w // HEAD_DIM),
                             jnp.tile(kn_g, a_w // HEAD_DIM)]).reshape(1, 2 * a_w)
    qk, conv_qk = _projection(_rope_epilogue, "projection_qk", u, w_in, 0, 2 * a_w, a_w,
                              [rope_spec, rope_spec, col_spec], [cos_full, sin_full, gains],
                              casts[0], tm=tm)
    vb_w = a_w + 3 * GROUP_W
    mult = jnp.ones((1, vb_w), F32).at[:, a_w:a_w + GROUP_W].set(QK_SCALE)
    vb, conv_vb = _projection(_scaled_epilogue, "projection_vb", u, w_in, 2 * a_w, vb_w, a_w,
                              [col_spec], [mult], casts[1], tm=tm)
    gates, conv_gates = _projection(_sigmoid_epilogue, "projection_gates", u, w_in,
                                    2 * a_w + vb_w, 2 * d_model, d_model, [], [], casts[2],
                                    tm=tm)
    return qk, vb, gates, (conv_qk, conv_vb, conv_gates)


def _residue_permutation(dil, transpose):
    n = PERM_BLOCK // dil
    a = lax.broadcasted_iota(jnp.int32, (PERM_BLOCK, PERM_BLOCK), 1 if transpose else 0)
    b = lax.broadcasted_iota(jnp.int32, (PERM_BLOCK, PERM_BLOCK), 0 if transpose else 1)
    return (b == (a % n) * dil + a // n).astype(BF16)


def _dilated_kernel(q_ref, k_ref, v_ref, o_ref, lse_ref, qs_ref, ks_ref, vs_ref, os_ref, ls_ref,
                    *, dil, rows_per_batch):
    seq = q_ref.shape[0]
    sub_len = seq // dil
    n = PERM_BLOCK // dil
    pad = WIN_BLOCK

    zeros = jnp.zeros((pad, GROUP_W), BF16)
    ks_ref[0:pad, :] = zeros
    vs_ref[0:pad, :] = zeros
    if dil == 1:
        ks_ref[pad:, :] = k_ref[...]
        vs_ref[pad:, :] = v_ref[...]
        q_src = q_ref
    else:
        perm = _residue_permutation(dil, transpose=False)
        for src, dst, off in ((q_ref, qs_ref, 0), (k_ref, ks_ref, pad), (v_ref, vs_ref, pad)):
            for nb in range(seq // PERM_BLOCK):
                y = _dot(perm, src[nb * PERM_BLOCK:(nb + 1) * PERM_BLOCK, :]).astype(BF16)
                for r in range(dil):
                    d0 = off + r * sub_len + nb * n
                    dst[d0:d0 + n, :] = y[r * n:(r + 1) * n, :]
        q_src = qs_ref
    o_dst, l_dst = (o_ref, lse_ref) if dil == 1 else (os_ref, ls_ref)

    qi_ = lax.broadcasted_iota(jnp.int32, (WIN_BLOCK, 2 * WIN_BLOCK), 0)
    kj_ = lax.broadcasted_iota(jnp.int32, (WIN_BLOCK, 2 * WIN_BLOCK), 1)
    band = (kj_ >= qi_) & (kj_ <= qi_ + WIN_BLOCK)
    band_first = band & (kj_ >= WIN_BLOCK)
    lane = lax.broadcasted_iota(jnp.int32, (WIN_BLOCK, HEAD_DIM), 1)
    n_qb = rows_per_batch // WIN_BLOCK
    for b0 in range(0, seq, rows_per_batch):
        units = [(b0 + qb * WIN_BLOCK, hh) for qb in range(n_qb) for hh in range(HEADS_PER_GROUP)]
        scores = []
        for r0, hh in units:
            cs = slice(hh * HEAD_DIM, (hh + 1) * HEAD_DIM)
            s = _dot_nt(q_src[r0:r0 + WIN_BLOCK, cs], ks_ref[r0:r0 + 2 * WIN_BLOCK, cs])
            mask = band_first if r0 % sub_len == 0 else band
            scores.append(jnp.where(mask, s, -jnp.inf))
        s_all = jnp.concatenate(scores, axis=0)
        m = jnp.max(s_all, axis=-1, keepdims=True)
        p = jnp.exp(s_all - m)
        den = jnp.sum(p, axis=-1, keepdims=True)
        p = p.astype(BF16)
        inv = 1.0 / den
        lse = m + jnp.log(den)
        for ui, (r0, hh) in enumerate(units):
            cs = slice(hh * HEAD_DIM, (hh + 1) * HEAD_DIM)
            us = slice(ui * WIN_BLOCK, (ui + 1) * WIN_BLOCK)
            pv = _dot(p[us], vs_ref[r0:r0 + 2 * WIN_BLOCK, cs])
            o_dst[r0:r0 + WIN_BLOCK, cs] = (pv * inv[us]).astype(BF16)
            if hh == 0:
                lse_tile = jnp.zeros((WIN_BLOCK, HEAD_DIM), F32)
            lse_tile = jnp.where(lane == hh, lse[us], lse_tile)
            if hh == HEADS_PER_GROUP - 1:
                l_dst[r0:r0 + WIN_BLOCK, :] = lse_tile

    if dil > 1:
        perm_t = _residue_permutation(dil, transpose=True)
        for nb in range(seq // PERM_BLOCK):
            rows = slice(nb * PERM_BLOCK, (nb + 1) * PERM_BLOCK)
            pieces = [slice(r * sub_len + nb * n, r * sub_len + (nb + 1) * n) for r in range(dil)]
            o_blk = jnp.concatenate([os_ref[pc, :] for pc in pieces], axis=0)
            o_ref[rows, :] = _dot(perm_t, o_blk).astype(BF16)
            for r, pc in enumerate(pieces):
                lse_ref[pl.ds(nb * PERM_BLOCK + r, n, stride=dil), :] = ls_ref[pc, :]


def _dilated_attention(qk, vb, *, bsz, seq, group, rows_per_batch=512):
    tokens = qk.shape[0]
    dil = DIL_PATTERNS[group][1]
    kern = functools.partial(_dilated_kernel, dil=dil, rows_per_batch=rows_per_batch)
    scratch = [
        pltpu.VMEM((seq, GROUP_W), BF16),
        pltpu.VMEM((seq + WIN_BLOCK, GROUP_W), BF16),
        pltpu.VMEM((seq + WIN_BLOCK, GROUP_W), BF16),
        pltpu.VMEM((seq, GROUP_W), BF16),
        pltpu.VMEM((seq, HEAD_DIM), F32),
    ]
    return pl.pallas_call(
        kern,
        grid=(bsz,),
        in_specs=[
            pl.BlockSpec((seq, GROUP_W), lambda b: (b, group)),
            pl.BlockSpec((seq, GROUP_W), lambda b: (b, N_GROUPS + group)),
            pl.BlockSpec((seq, GROUP_W), lambda b: (b, group)),
        ],
        out_specs=[
            pl.BlockSpec((seq, GROUP_W), lambda b: (b, 0)),
            pl.BlockSpec((seq, HEAD_DIM), lambda b: (b, 0)),
        ],
        out_shape=[
            jax.ShapeDtypeStruct((tokens, GROUP_W), BF16),
            jax.ShapeDtypeStruct((tokens, HEAD_DIM), F32),
        ],
        scratch_shapes=scratch,
        compiler_params=_params("parallel"),
        name=f"dilated_attention_g{group}",
    )(qk, qk, vb)


def _sb_kernel(q_ref, k_ref, v_ref, *refs, tq, tk, n_cast):
    cast_in, o_ref, cast_out = refs[:n_cast], refs[n_cast], refs[n_cast + 1:]
    _run_casts(cast_in, cast_out)
    qi = pl.program_id(1)
    heads = q_ref.shape[1] // HEAD_DIM
    n_seg = tq // tk
    head_cols = [slice(hh * HEAD_DIM, (hh + 1) * HEAD_DIM) for hh in range(heads)]
    r = lax.broadcasted_iota(jnp.int32, (tk, tk), 0)
    c = lax.broadcasted_iota(jnp.int32, (tk, tk), 1)
    suffix = (r > c).astype(BF16)
    q_row = lax.broadcasted_iota(jnp.int32, (heads * tk, tk), 0) % tk
    k_col = lax.broadcasted_iota(jnp.int32, (heads * tk, tk), 1)
    strictly_before = k_col < q_row

    def block(kblk, r0, r1, on_diagonal):
        rows = pl.ds(pl.multiple_of(kblk * tk, tk), tk)
        n = r1 - r0
        z = jnp.concatenate([_dot_nt(q_ref[r0:r1, cs], k_ref[rows, cs]) for cs in head_cols],
                            axis=0)
        sp = jnp.log(1.0 + jnp.exp(-jnp.abs(z))) + jnp.maximum(z, 0.0)
        log_beta = z - sp
        if on_diagonal:
            sp = jnp.where(strictly_before, sp, 0.0)
        local = _dot(sp.astype(BF16), suffix)
        a = jnp.exp(log_beta - local)
        if on_diagonal:
            a = jnp.where(strictly_before, a, 0.0)
        a = a.astype(BF16)
        pv = jnp.concatenate([_dot(a[hh * n:(hh + 1) * n], v_ref[rows, cs])
                              for hh, cs in enumerate(head_cols)], axis=0)
        return pv, local[:, 0:1] + sp[:, 0:1]

    def absorb(acc, later, r0, r1, pv, total):
        if (r0, r1) == (0, tq):
            return acc + jnp.exp(-later) * pv, later + total
        n = r1 - r0
        acc_parts, later_parts = [], []
        for hh in range(heads):
            lo, hi = hh * tq + r0, hh * tq + r1
            piece = slice(hh * n, (hh + 1) * n)
            for parts, old, new in (
                    (acc_parts, acc, acc[lo:hi] + jnp.exp(-later[lo:hi]) * pv[piece]),
                    (later_parts, later, later[lo:hi] + total[piece])):
                parts += [old[hh * tq:lo]] if r0 else []
                parts.append(new)
                parts += [old[hi:(hh + 1) * tq]] if r1 < tq else []
        return jnp.concatenate(acc_parts, axis=0), jnp.concatenate(later_parts, axis=0)

    acc = jnp.zeros((heads * tq, HEAD_DIM), F32)
    later = jnp.zeros((heads * tq, 1), F32)
    for d in reversed(range(n_seg)):
        r0, r1 = d * tk, (d + 1) * tk
        acc, later = absorb(acc, later, r0, r1, *block(qi * n_seg + d, r0, r1, True))
        if r1 < tq:
            acc, later = absorb(acc, later, r1, tq, *block(qi * n_seg + d, r1, tq, False))

    def below_diagonal(t, carry):
        acc, later = carry
        for d in range(n_seg):
            acc, later = absorb(acc, later, 0, tq,
                                *block((qi - t) * n_seg - 1 - d, 0, tq, False))
        return acc, later

    acc, later = lax.fori_loop(0, qi, below_diagonal, (acc, later))
    for hh, cs in enumerate(head_cols):
        o_ref[:, cs] = acc[hh * tq:(hh + 1) * tq].astype(BF16)


def _stick_breaking(vb, casts=(), *, bsz, seq, tq=512, tk=256):
    tokens = vb.shape[0]
    nq = seq // tq
    cast_in, cast_out, cast_shapes = _cast_specs(casts, bsz * nq, lambda b, i: b * nq + i)
    kern = functools.partial(_sb_kernel, tq=tq, tk=tk, n_cast=len(casts))
    out, *converted = pl.pallas_call(
        kern,
        grid=(bsz, nq),
        in_specs=[
            pl.BlockSpec((tq, GROUP_W), lambda b, i: (b * nq + i, N_GROUPS)),
            pl.BlockSpec((seq, GROUP_W), lambda b, i: (b, N_GROUPS + 1)),
            pl.BlockSpec((seq, GROUP_W), lambda b, i: (b, N_GROUPS + 2)),
        ] + cast_in,
        out_specs=[pl.BlockSpec((tq, GROUP_W), lambda b, i: (b * nq + i, 0))] + cast_out,
        out_shape=[jax.ShapeDtypeStruct((tokens, GROUP_W), BF16)] + cast_shapes,
        compiler_params=_params("arbitrary", "arbitrary"),
        name="stick_breaking",
    )(vb, vb, vb, *[w_f32 for w_f32, _ in casts])
    return out, converted


def _merge_kernel(h_ref, o0_ref, o1_ref, o2_ref, l0_ref, l1_ref, l2_ref, ob_ref,
                  sga_ref, sgb_ref, gate_ref, g2_ref, scale2_ref, shift2_ref,
                  wa_ref, wb_ref, wo_ref, out_ref, u2_ref, oa_ref):
    tm = h_ref.shape[0]
    parts = [slice(i * (tm // MERGE_PARTS), (i + 1) * (tm // MERGE_PARTS))
             for i in range(MERGE_PARTS)]

    def combine(rs):
        l0, l1, l2 = l0_ref[rs, :], l1_ref[rs, :], l2_ref[rs, :]
        m = jnp.maximum(jnp.maximum(l0, l1), l2)
        e0 = jnp.exp(l0 - m)
        e1 = jnp.exp(l1 - m)
        e2 = jnp.exp(l2 - m)
        inv = 1.0 / (e0 + e1 + e2)
        w0, w1, w2 = e0 * inv, e1 * inv, e2 * inv
        for hh in range(HEADS_PER_GROUP):
            cs = slice(hh * HEAD_DIM, (hh + 1) * HEAD_DIM)
            oa = (w0[:, hh:hh + 1] * o0_ref[rs, cs].astype(F32)
                  + w1[:, hh:hh + 1] * o1_ref[rs, cs].astype(F32)
                  + w2[:, hh:hh + 1] * o2_ref[rs, cs].astype(F32))
            oa_ref[rs, cs] = oa.astype(BF16)

    def project(rs):
        y_a = _dot(oa_ref[rs, :], wa_ref[...])
        y_b = _dot(ob_ref[rs, :], wb_ref[...])
        merged = sga_ref[rs, :].astype(F32) * y_a + sgb_ref[rs, :].astype(F32) * y_b
        h_new = h_ref[rs, :] + gate_ref[...] * _dot(merged.astype(BF16), wo_ref[...])
        out_ref[rs, :] = h_new
        return h_new

    def norm(rs, h_new):
        u2_ref[rs, :] = _modulated_norm(h_new, g2_ref[...], scale2_ref[...],
                                        shift2_ref[...]).astype(BF16)

    for rs in parts:
        combine(rs)
    for rs, h_new in [(rs, project(rs)) for rs in parts]:
        norm(rs, h_new)


def _merge(h, o_groups, lse_groups, o_b, gates, mod3, gate_idx, g2, shift2_idx, scale2_idx,
           wa, wb, wo, *, seq, tm=512):
    tokens, d_model = h.shape
    blocks_per_seq = seq // tm
    row = lambda i: (i, 0)
    resident = lambda w: pl.BlockSpec(w.shape, lambda i: (0, 0), pipeline_mode=pl.Buffered(1))
    mod_spec = lambda idx: pl.BlockSpec((None, 1, d_model),
                                        lambda i: (i // blocks_per_seq, 0, idx))
    return pl.pallas_call(
        _merge_kernel,
        grid=(tokens // tm,),
        in_specs=[
            pl.BlockSpec((tm, d_model), row),
            pl.BlockSpec((tm, GROUP_W), row),
            pl.BlockSpec((tm, GROUP_W), row),
            pl.BlockSpec((tm, GROUP_W), row),
            pl.BlockSpec((tm, HEAD_DIM), row),
            pl.BlockSpec((tm, HEAD_DIM), row),
            pl.BlockSpec((tm, HEAD_DIM), row),
            pl.BlockSpec((tm, GROUP_W), row),
            pl.BlockSpec((tm, d_model), lambda i: (i, 0)),
            pl.BlockSpec((tm, d_model), lambda i: (i, 1)),
            mod_spec(gate_idx),
            pl.BlockSpec((1, d_model), lambda i: (0, 0)),
            mod_spec(scale2_idx),
            mod_spec(shift2_idx),
            resident(wa),
            resident(wb),
            resident(wo),
        ],
        out_specs=[pl.BlockSpec((tm, d_model), row), pl.BlockSpec((tm, d_model), row)],
        out_shape=[jax.ShapeDtypeStruct((tokens, d_model), F32),
                   jax.ShapeDtypeStruct((tokens, d_model), BF16)],
        scratch_shapes=[pltpu.VMEM((tm, GROUP_W), BF16)],
        compiler_params=_params("parallel"),
        name="merge",
    )(h, *o_groups, *lse_groups, o_b, gates, gates, mod3, g2, mod3, mod3, wa, wb, wo)


def _ffn_kernel(u_ref, hres_ref, gate_ref, wg_ref, wu_ref, wd_ref, *refs, nf, tf, n_cast):
    cast_in, out_ref, cast_out, act_ref = refs[:n_cast], refs[n_cast], refs[n_cast + 1:-1], refs[-1]
    s = pl.program_id(1)

    @pl.when(s < nf)
    def _():
        u = u_ref[...]
        gate_act = _dot(u, wg_ref[...])
        up = _dot(u, wu_ref[...])
        cols = pl.ds(pl.multiple_of(s * tf, tf), tf)
        act_ref[:, cols] = (gate_act * jax.nn.sigmoid(gate_act) * up).astype(BF16)
        _run_casts(cast_in, cast_out)

    @pl.when(s >= nf)
    def _():
        out_ref[...] = hres_ref[...] + gate_ref[...] * _dot(act_ref[...], wd_ref[...])
        _run_casts(cast_in, cast_out)


def _ffn(h, u2, mod3, gate_idx, w_gate_up, w_down, casts=(), *, seq, tm=1024, tf=512, tn=512):
    tokens, d_model = h.shape
    d_ff = w_down.shape[0]
    nf = d_ff // tf
    nd = d_model // tn
    steps = nf + nd
    blocks_per_seq = seq // tm
    up_step = lambda s: jnp.minimum(s, nf - 1)
    down_step = lambda s: jnp.clip(s - nf, 0, nd - 1)
    cast_in, cast_out, cast_shapes = _cast_specs(casts, (tokens // tm) * steps,
                                                 lambda i, s: i * steps + s)
    kern = functools.partial(_ffn_kernel, nf=nf, tf=tf, n_cast=len(casts))
    out, *converted = pl.pallas_call(
        kern,
        grid=(tokens // tm, steps),
        in_specs=[
            pl.BlockSpec((tm, d_model), lambda i, s: (i, 0)),
            pl.BlockSpec((tm, tn), lambda i, s: (i, down_step(s))),
            pl.BlockSpec((None, 1, tn),
                         lambda i, s: (i // blocks_per_seq, 0, gate_idx * nd + down_step(s))),
            pl.BlockSpec((d_model, tf), lambda i, s: (0, up_step(s))),
            pl.BlockSpec((d_model, tf), lambda i, s: (0, nf + up_step(s))),
            pl.BlockSpec((d_ff, tn), lambda i, s: (0, down_step(s))),
        ] + cast_in,
        out_specs=[pl.BlockSpec((tm, tn), lambda i, s: (i, down_step(s)))] + cast_out,
        out_shape=[jax.ShapeDtypeStruct((tokens, d_model), F32)] + cast_shapes,
        scratch_shapes=[pltpu.VMEM((tm, d_ff), BF16)],
        compiler_params=_params("arbitrary", "arbitrary"),
        name="ffn",
    )(u2, h, mod3, w_gate_up, w_gate_up, w_down, *[w_f32 for w_f32, _ in casts])
    return out, converted


def _rope_tables(seq):
    inv = jnp.power(ROPE_THETA, -jnp.arange(0, HEAD_DIM, 2, dtype=F32) / HEAD_DIM)
    ang = jnp.arange(seq, dtype=F32)[:, None] * inv[None, :]
    cos, sin = jnp.cos(ang), jnp.sin(ang)
    return jnp.concatenate([cos, cos], axis=-1), jnp.concatenate([-sin, sin], axis=-1)


def kernel(x, c, w_ada, b_ada, norm1_g, norm2_g, w_in, qn_g, kn_g, w_branch_a, w_branch_b,
           w_out, w_gate_up, w_down):
    bsz, seq, d_model = x.shape
    depth = w_ada.shape[0]
    tokens = bsz * seq
    cos_full, sin_full = _rope_tables(seq)
    mod = _modulation(c, w_ada, b_ada)
    w_in_l = w_in[0].astype(BF16)
    h = x.reshape(tokens, d_model)
    for l in range(depth):
        mod3 = mod[l].reshape(bsz, 1, 6 * d_model)
        u = _norm_modulate(h, norm1_g[l].reshape(1, d_model), mod3, 0, 1, seq)
        has_next = l + 1 < depth
        nxt = lambda *ws: [(w, l + 1) for w in ws] if has_next else []
        first = lambda *ws: [(w, 0) for w in ws] if l == 0 else []
        casts = (first(w_gate_up), first(w_down),
                 first(w_out, w_branch_a, w_branch_b) + nxt(w_in))
        qk, vb, gates, converted = _input_projections(u, w_in_l, cos_full, sin_full, qn_g[l],
                                                      kn_g[l], casts, seq=seq)
        if l == 0:
            (w_gate_up_l,), (w_down_l,) = converted[:2]
            w_out_l, w_branch_a_l, w_branch_b_l = converted[2][:3]
        outs = [_dilated_attention(qk, vb, bsz=bsz, seq=seq, group=g) for g in range(N_GROUPS)]
        o_b, next_gate_up = _stick_breaking(vb, nxt(w_gate_up), bsz=bsz, seq=seq)
        h, u2 = _merge(h, [o for o, _ in outs], [s for _, s in outs], o_b, gates, mod3, 2,
                       norm2_g[l].reshape(1, d_model), 3, 4,
                       w_branch_a_l, w_branch_b_l, w_out_l, seq=seq)
        h, next_rest = _ffn(h, u2, mod3, 5, w_gate_up_l, w_down_l,
                            nxt(w_down, w_out, w_branch_a, w_branch_b), seq=seq)
        if has_next:
            w_in_l, (w_gate_up_l,) = converted[2][-1], next_gate_up
            w_down_l, w_out_l, w_branch_a_l, w_branch_b_l = next_rest
    return h.reshape(bsz, seq, d_model)
```

```python
import functools

import jax
import jax.numpy as jnp
from jax import lax
from jax.experimental import pallas as pl
from jax.experimental.pallas import tpu as pltpu

F32 = jnp.float32
BF16 = jnp.bfloat16

HEAD_DIM = 128
HEADS_PER_GROUP = 4
GROUP_W = HEADS_PER_GROUP * HEAD_DIM
DIL_PATTERNS = ((128, 1), (512, 4), (2048, 16))
N_GROUPS = len(DIL_PATTERNS)
WIN_BLOCK = 128
PERM_BLOCK = 256
BF16_SUBLANES = 16
MERGE_PARTS = 4
MXU_WIDTH = 256
ROPE_THETA = 10000.0
EPS = 1e-6
QK_SCALE = HEAD_DIM ** -0.5

VMEM_LIMIT_BYTES = 56 * 1024 * 1024


def _params(*semantics):
    return pltpu.CompilerParams(dimension_semantics=semantics,
                                vmem_limit_bytes=VMEM_LIMIT_BYTES)


def _dot(a, b):
    return jnp.dot(a, b, preferred_element_type=F32)


def _dot_nt(a, b):
    return lax.dot_general(a, b, (((1,), (1,)), ((), ())), preferred_element_type=F32)


def _mod_kernel(c_ref, w_ref, b_ref, o_ref):
    c = c_ref[...]
    c_act = (c * jax.nn.sigmoid(c)).astype(BF16)
    o_ref[...] = _dot(c_act, w_ref[...].astype(BF16)) + b_ref[...]


def _modulation(c, w_ada, b_ada, tn=1024):
    depth, d_model, n = w_ada.shape
    bsz = c.shape[0]
    return pl.pallas_call(
        _mod_kernel,
        grid=(depth, n // tn),
        in_specs=[
            pl.BlockSpec((bsz, d_model), lambda l, j: (0, 0)),
            pl.BlockSpec((None, d_model, tn), lambda l, j: (l, 0, j)),
            pl.BlockSpec((None, 1, tn), lambda l, j: (l, 0, j)),
        ],
        out_specs=pl.BlockSpec((None, bsz, tn), lambda l, j: (l, 0, j)),
        out_shape=jax.ShapeDtypeStruct((depth, bsz, n), F32),
        compiler_params=_params("parallel", "parallel"),
        name="modulation",
    )(c, w_ada, b_ada.reshape(depth, 1, n))


def _modulated_norm(h, g, scale, shift):
    ms = jnp.mean(h * h, axis=-1, keepdims=True)
    return (h * lax.rsqrt(ms + EPS)) * (g * (1.0 + scale)) + shift


def _norm_kernel(h_ref, g_ref, scale_ref, shift_ref, o_ref):
    o_ref[...] = _modulated_norm(h_ref[...], g_ref[...], scale_ref[...],
                                 shift_ref[...]).astype(BF16)


def _norm_modulate(h, g, mod3, shift_idx, scale_idx, seq, tm=1024):
    tokens, d_model = h.shape
    blocks_per_seq = seq // tm
    return pl.pallas_call(
        _norm_kernel,
        grid=(tokens // tm,),
        in_specs=[
            pl.BlockSpec((tm, d_model), lambda i: (i, 0)),
            pl.BlockSpec((1, d_model), lambda i: (0, 0)),
            pl.BlockSpec((None, 1, d_model), lambda i: (i // blocks_per_seq, 0, scale_idx)),
            pl.BlockSpec((None, 1, d_model), lambda i: (i // blocks_per_seq, 0, shift_idx)),
        ],
        out_specs=pl.BlockSpec((tm, d_model), lambda i: (i, 0)),
        out_shape=jax.ShapeDtypeStruct((tokens, d_model), BF16),
        compiler_params=_params("parallel"),
        name="norm_modulate",
    )(h, g, mod3, mod3)


def _rope_epilogue(acc, cos_ref, sin_ref, gain_ref, o_ref):
    cos = cos_ref[...]
    sin = sin_ref[...]
    for hh in range(acc.shape[1] // HEAD_DIM):
        cs = slice(hh * HEAD_DIM, (hh + 1) * HEAD_DIM)
        x = acc[:, cs]
        ms = jnp.mean(x * x, axis=-1, keepdims=True)
        y = (x * lax.rsqrt(ms + EPS)) * gain_ref[:, cs]
        o_ref[:, cs] = (y * cos + pltpu.roll(y, HEAD_DIM // 2, 1) * sin).astype(BF16)


def _scaled_epilogue(acc, mult_ref, o_ref):
    o_ref[...] = (acc * mult_ref[...]).astype(BF16)


def _sigmoid_epilogue(acc, o_ref):
    o_ref[...] = jax.nn.sigmoid(acc).astype(BF16)


def _cast_specs(casts, n_steps, step_of):
    in_specs, out_specs, out_shapes = [], [], []
    for w, layer in casts:
        _, n_rows, n_cols = w.shape
        rows = next(r for r in range(BF16_SUBLANES, n_rows + 1, BF16_SUBLANES)
                    if n_rows % r == 0 and n_rows // r <= n_steps)
        chunk = lambda *ids, last=n_rows // rows - 1: jnp.minimum(step_of(*ids), last)
        in_specs.append(pl.BlockSpec((None, rows, n_cols),
                                     lambda *ids, layer=layer, chunk=chunk: (layer, chunk(*ids), 0)))
        out_specs.append(pl.BlockSpec((rows, n_cols), lambda *ids, chunk=chunk: (chunk(*ids), 0)))
        out_shapes.append(jax.ShapeDtypeStruct((n_rows, n_cols), BF16))
    return in_specs, out_specs, out_shapes


def _run_casts(src_refs, dst_refs):
    for src, dst in zip(src_refs, dst_refs):
        dst[...] = src[...].astype(BF16)


def _proj_kernel(u_ref, w_ref, *refs, epilogue, n_cast):
    n_extra = len(refs) - 2 * n_cast - 2
    extra_refs, cast_in = refs[:n_extra], refs[n_extra:n_extra + n_cast]
    o_ref, cast_out, acc_ref = refs[n_extra + n_cast], refs[n_extra + n_cast + 1:-1], refs[-1]

    @pl.when(pl.program_id(0) == 0)
    def _():
        acc_ref[...] = jnp.zeros(acc_ref.shape, F32)

    epilogue(acc_ref[...], *extra_refs, o_ref)
    acc_ref[...] = _dot(u_ref[...], w_ref[...])
    _run_casts(cast_in, cast_out)


def _projection(epilogue, name, u, w, first_col, n_cols, tile_w, extra_specs, extra_args,
                casts=(), tm=1024):
    tokens, d_model = u.shape
    n_tiles = n_cols // tile_w
    first_tile = first_col // tile_w
    n_pairs = (tokens // tm) * n_tiles
    mm = lambda t: jnp.minimum(t, n_pairs - 1)
    ep = lambda t: jnp.maximum(t - 1, 0)
    lagged = lambda spec: pl.BlockSpec(
        spec.block_shape, lambda t, f=spec.index_map: f(ep(t) // n_tiles, ep(t) % n_tiles))
    cast_in, cast_out, cast_shapes = _cast_specs(casts, n_pairs + 1, lambda t: t)
    out, *converted = pl.pallas_call(
        functools.partial(_proj_kernel, epilogue=epilogue, n_cast=len(casts)),
        grid=(n_pairs + 1,),
        in_specs=[
            pl.BlockSpec((tm, d_model), lambda t: (mm(t) // n_tiles, 0)),
            pl.BlockSpec((d_model, tile_w), lambda t: (0, first_tile + mm(t) % n_tiles)),
        ] + [lagged(spec) for spec in extra_specs] + cast_in,
        out_specs=[lagged(pl.BlockSpec((tm, tile_w), lambda i, j: (i, j)))] + cast_out,
        out_shape=[jax.ShapeDtypeStruct((tokens, n_cols), BF16)] + cast_shapes,
        scratch_shapes=[pltpu.VMEM((tm, tile_w), F32)],
        compiler_params=_params("arbitrary"),
        name=name,
    )(u, w, *extra_args, *[w_f32 for w_f32, _ in casts])
    return out, converted


def _input_projections(u, w_in, cos_full, sin_full, qn_g, kn_g, casts, *, seq, tm=1024):
    d_model = u.shape[1]
    blocks_per_seq = seq // tm
    a_w = N_GROUPS * GROUP_W
    col_spec = pl.BlockSpec((1, a_w), lambda i, j: (0, j))
    rope_spec = pl.BlockSpec((tm, HEAD_DIM), lambda i, j: (i % blocks_per_seq, 0))
    gains = jnp.concatenate([jnp.tile(qn_g * QK_SCALE, a_w // HEAD_DIM),
                             jnp.tile(kn_g, a_w // HEAD_DIM)]).reshape(1, 2 * a_w)
    qk, conv_qk = _projection(_rope_epilogue, "projection_qk", u, w_in, 0, 2 * a_w, a_w,
                              [rope_spec, rope_spec, col_spec], [cos_full, sin_full, gains],
                              casts[0], tm=tm)
    vb_w = a_w + 3 * GROUP_W
    mult = jnp.ones((1, vb_w), F32).at[:, a_w:a_w + GROUP_W].set(QK_SCALE)
    vb, conv_vb = _projection(_scaled_epilogue, "projection_vb", u, w_in, 2 * a_w, vb_w, a_w,
                              [col_spec], [mult], casts[1], tm=tm)
    gates, conv_gates = _projection(_sigmoid_epilogue, "projection_gates", u, w_in,
                                    2 * a_w + vb_w, 2 * d_model, d_model, [], [], casts[2],
                                    tm=tm)
    return qk, vb, gates, (conv_qk, conv_vb, conv_gates)


def _residue_permutation(dil, transpose):
    n = PERM_BLOCK // dil
    a = lax.broadcasted_iota(jnp.int32, (PERM_BLOCK, PERM_BLOCK), 1 if transpose else 0)
    b = lax.broadcasted_iota(jnp.int32, (PERM_BLOCK, PERM_BLOCK), 0 if transpose else 1)
    return (b == (a % n) * dil + a // n).astype(BF16)


def _dilated_kernel(q_ref, k_ref, v_ref, o_ref, lse_ref, qs_ref, ks_ref, vs_ref, os_ref, ls_ref,
                    *, dil, rows_per_batch):
    seq = q_ref.shape[0]
    sub_len = seq // dil
    n = PERM_BLOCK // dil
    pad = WIN_BLOCK

    zeros = jnp.zeros((pad, GROUP_W), BF16)
    ks_ref[0:pad, :] = zeros
    vs_ref[0:pad, :] = zeros
    if dil == 1:
        ks_ref[pad:, :] = k_ref[...]
        vs_ref[pad:, :] = v_ref[...]
        q_src = q_ref
    else:
        perm = _residue_permutation(dil, transpose=False)
        for src, dst, off in ((q_ref, qs_ref, 0), (k_ref, ks_ref, pad), (v_ref, vs_ref, pad)):
            for nb in range(seq // PERM_BLOCK):
                y = _dot(perm, src[nb * PERM_BLOCK:(nb + 1) * PERM_BLOCK, :]).astype(BF16)
                for r in range(dil):
                    d0 = off + r * sub_len + nb * n
                    dst[d0:d0 + n, :] = y[r * n:(r + 1) * n, :]
        q_src = qs_ref
    o_dst, l_dst = (o_ref, lse_ref) if dil == 1 else (os_ref, ls_ref)

    qi_ = lax.broadcasted_iota(jnp.int32, (WIN_BLOCK, 2 * WIN_BLOCK), 0)
    kj_ = lax.broadcasted_iota(jnp.int32, (WIN_BLOCK, 2 * WIN_BLOCK), 1)
    band = (kj_ >= qi_) & (kj_ <= qi_ + WIN_BLOCK)
    band_first = band & (kj_ >= WIN_BLOCK)
    lane = lax.broadcasted_iota(jnp.int32, (WIN_BLOCK, HEAD_DIM), 1)
    n_qb = rows_per_batch // WIN_BLOCK
    for b0 in range(0, seq, rows_per_batch):
        units = [(b0 + qb * WIN_BLOCK, hh) for qb in range(n_qb) for hh in range(HEADS_PER_GROUP)]
        scores = []
        for r0, hh in units:
            cs = slice(hh * HEAD_DIM, (hh + 1) * HEAD_DIM)
            s = _dot_nt(q_src[r0:r0 + WIN_BLOCK, cs], ks_ref[r0:r0 + 2 * WIN_BLOCK, cs])
            mask = band_first if r0 % sub_len == 0 else band
            scores.append(jnp.where(mask, s, -jnp.inf))
        s_all = jnp.concatenate(scores, axis=0)
        m = jnp.max(s_all, axis=-1, keepdims=True)
        p = jnp.exp(s_all - m)
        den = jnp.sum(p, axis=-1, keepdims=True)
        p = p.astype(BF16)
        inv = 1.0 / den
        lse = m + jnp.log(den)
        for ui, (r0, hh) in enumerate(units):
            cs = slice(hh * HEAD_DIM, (hh + 1) * HEAD_DIM)
            us = slice(ui * WIN_BLOCK, (ui + 1) * WIN_BLOCK)
            pv = _dot(p[us], vs_ref[r0:r0 + 2 * WIN_BLOCK, cs])
            o_dst[r0:r0 + WIN_BLOCK, cs] = (pv * inv[us]).astype(BF16)
            if hh == 0:
                lse_tile = jnp.zeros((WIN_BLOCK, HEAD_DIM), F32)
            lse_tile = jnp.where(lane == hh, lse[us], lse_tile)
            if hh == HEADS_PER_GROUP - 1:
                l_dst[r0:r0 + WIN_BLOCK, :] = lse_tile

    if dil > 1:
        perm_t = _residue_permutation(dil, transpose=True)
        for nb in range(seq // PERM_BLOCK):
            rows = slice(nb * PERM_BLOCK, (nb + 1) * PERM_BLOCK)
            pieces = [slice(r * sub_len + nb * n, r * sub_len + (nb + 1) * n) for r in range(dil)]
            o_blk = jnp.concatenate([os_ref[pc, :] for pc in pieces], axis=0)
            o_ref[rows, :] = _dot(perm_t, o_blk).astype(BF16)
            for r, pc in enumerate(pieces):
                lse_ref[pl.ds(nb * PERM_BLOCK + r, n, stride=dil), :] = ls_ref[pc, :]


def _dilated_attention(qk, vb, *, bsz, seq, group, rows_per_batch=512):
    tokens = qk.shape[0]
    dil = DIL_PATTERNS[group][1]
    kern = functools.partial(_dilated_kernel, dil=dil, rows_per_batch=rows_per_batch)
    scratch = [
        pltpu.VMEM((seq, GROUP_W), BF16),
        pltpu.VMEM((seq + WIN_BLOCK, GROUP_W), BF16),
        pltpu.VMEM((seq + WIN_BLOCK, GROUP_W), BF16),
        pltpu.VMEM((seq, GROUP_W), BF16),
        pltpu.VMEM((seq, HEAD_DIM), F32),
    ]
    return pl.pallas_call(
        kern,
        grid=(bsz,),
        in_specs=[
            pl.BlockSpec((seq, GROUP_W), lambda b: (b, group)),
            pl.BlockSpec((seq, GROUP_W), lambda b: (b, N_GROUPS + group)),
            pl.BlockSpec((seq, GROUP_W), lambda b: (b, group)),
        ],
        out_specs=[
            pl.BlockSpec((seq, GROUP_W), lambda b: (b, 0)),
            pl.BlockSpec((seq, HEAD_DIM), lambda b: (b, 0)),
        ],
        out_shape=[
            jax.ShapeDtypeStruct((tokens, GROUP_W), BF16),
            jax.ShapeDtypeStruct((tokens, HEAD_DIM), F32),
        ],
        scratch_shapes=scratch,
        compiler_params=_params("parallel"),
        name=f"dilated_attention_g{group}",
    )(qk, qk, vb)


def _sb_kernel(q_ref, k_ref, v_ref, *refs, tq, tk, n_cast):
    cast_in, o_ref, cast_out = refs[:n_cast], refs[n_cast], refs[n_cast + 1:]
    _run_casts(cast_in, cast_out)
    qi = pl.program_id(1)
    heads = q_ref.shape[1] // HEAD_DIM
    n_seg = tq // tk
    head_cols = [slice(hh * HEAD_DIM, (hh + 1) * HEAD_DIM) for hh in range(heads)]
    r = lax.broadcasted_iota(jnp.int32, (tk, tk), 0)
    c = lax.broadcasted_iota(jnp.int32, (tk, tk), 1)
    suffix = (r > c).astype(BF16)
    q_row = lax.broadcasted_iota(jnp.int32, (heads * tk, tk), 0) % tk
    k_col = lax.broadcasted_iota(jnp.int32, (heads * tk, tk), 1)
    strictly_before = k_col < q_row

    def block(kblk, r0, r1, on_diagonal):
        rows = pl.ds(pl.multiple_of(kblk * tk, tk), tk)
        n = r1 - r0
        z = jnp.concatenate([_dot_nt(q_ref[r0:r1, cs], k_ref[rows, cs]) for cs in head_cols],
                            axis=0)
        sp = jnp.log(1.0 + jnp.exp(-jnp.abs(z))) + jnp.maximum(z, 0.0)
        log_beta = z - sp
        if on_diagonal:
            sp = jnp.where(strictly_before, sp, 0.0)
        local = _dot(sp.astype(BF16), suffix)
        a = jnp.exp(log_beta - local)
        if on_diagonal:
            a = jnp.where(strictly_before, a, 0.0)
        a = a.astype(BF16)
        pv = jnp.concatenate([_dot(a[hh * n:(hh + 1) * n], v_ref[rows, cs])
                              for hh, cs in enumerate(head_cols)], axis=0)
        return pv, local[:, 0:1] + sp[:, 0:1]

    def absorb(acc, later, r0, r1, pv, total):
        if (r0, r1) == (0, tq):
            return acc + jnp.exp(-later) * pv, later + total
        n = r1 - r0
        acc_parts, later_parts = [], []
        for hh in range(heads):
            lo, hi = hh * tq + r0, hh * tq + r1
            piece = slice(hh * n, (hh + 1) * n)
            for parts, old, new in (
                    (acc_parts, acc, acc[lo:hi] + jnp.exp(-later[lo:hi]) * pv[piece]),
                    (later_parts, later, later[lo:hi] + total[piece])):
                parts += [old[hh * tq:lo]] if r0 else []
                parts.append(new)
                parts += [old[hi:(hh + 1) * tq]] if r1 < tq else []
        return jnp.concatenate(acc_parts, axis=0), jnp.concatenate(later_parts, axis=0)

    acc = jnp.zeros((heads * tq, HEAD_DIM), F32)
    later = jnp.zeros((heads * tq, 1), F32)
    for d in reversed(range(n_seg)):
        r0, r1 = d * tk, (d + 1) * tk
        acc, later = absorb(acc, later, r0, r1, *block(qi * n_seg + d, r0, r1, True))
        if r1 < tq:
            acc, later = absorb(acc, later, r1, tq, *block(qi * n_seg + d, r1, tq, False))

    def below_diagonal(t, carry):
        acc, later = carry
        for d in range(n_seg):
            acc, later = absorb(acc, later, 0, tq,
                                *block((qi - t) * n_seg - 1 - d, 0, tq, False))
        return acc, later

    acc, later = lax.fori_loop(0, qi, below_diagonal, (acc, later))
    for hh, cs in enumerate(head_cols):
        o_ref[:, cs] = acc[hh * tq:(hh + 1) * tq].astype(BF16)


def _stick_breaking(vb, casts=(), *, bsz, seq, tq=512, tk=256):
    tokens = vb.shape[0]
    nq = seq // tq
    cast_in, cast_out, cast_shapes = _cast_specs(casts, bsz * nq, lambda b, i: b * nq + i)
    kern = functools.partial(_sb_kernel, tq=tq, tk=tk, n_cast=len(casts))
    out, *converted = pl.pallas_call(
        kern,
        grid=(bsz, nq),
        in_specs=[
            pl.BlockSpec((tq, GROUP_W), lambda b, i: (b * nq + i, N_GROUPS)),
            pl.BlockSpec((seq, GROUP_W), lambda b, i: (b, N_GROUPS + 1)),
            pl.BlockSpec((seq, GROUP_W), lambda b, i: (b, N_GROUPS + 2)),
        ] + cast_in,
        out_specs=[pl.BlockSpec((tq, GROUP_W), lambda b, i: (b * nq + i, 0))] + cast_out,
        out_shape=[jax.ShapeDtypeStruct((tokens, GROUP_W), BF16)] + cast_shapes,
        compiler_params=_params("arbitrary", "arbitrary"),
        name="stick_breaking",
    )(vb, vb, vb, *[w_f32 for w_f32, _ in casts])
    return out, converted


def _merge_kernel(h_ref, o0_ref, o1_ref, o2_ref, l0_ref, l1_ref, l2_ref, ob_ref,
                  sga_ref, sgb_ref, gate_ref, g2_ref, scale2_ref, shift2_ref,
                  wa_ref, wb_ref, wo_ref, out_ref, u2_ref, oa_ref):
    tm = h_ref.shape[0]
    parts = [slice(i * (tm // MERGE_PARTS), (i + 1) * (tm // MERGE_PARTS))
             for i in range(MERGE_PARTS)]

    def combine(rs):
        l0, l1, l2 = l0_ref[rs, :], l1_ref[rs, :], l2_ref[rs, :]
        m = jnp.maximum(jnp.maximum(l0, l1), l2)
        e0 = jnp.exp(l0 - m)
        e1 = jnp.exp(l1 - m)
        e2 = jnp.exp(l2 - m)
        inv = 1.0 / (e0 + e1 + e2)
        w0, w1, w2 = e0 * inv, e1 * inv, e2 * inv
        for hh in range(HEADS_PER_GROUP):
            cs = slice(hh * HEAD_DIM, (hh + 1) * HEAD_DIM)
            oa = (w0[:, hh:hh + 1] * o0_ref[rs, cs].astype(F32)
                  + w1[:, hh:hh + 1] * o1_ref[rs, cs].astype(F32)
                  + w2[:, hh:hh + 1] * o2_ref[rs, cs].astype(F32))
            oa_ref[rs, cs] = oa.astype(BF16)

    def project(rs):
        y_a = _dot(oa_ref[rs, :], wa_ref[...])
        y_b = _dot(ob_ref[rs, :], wb_ref[...])
        merged = sga_ref[rs, :].astype(F32) * y_a + sgb_ref[rs, :].astype(F32) * y_b
        h_new = h_ref[rs, :] + gate_ref[...] * _dot(merged.astype(BF16), wo_ref[...])
        out_ref[rs, :] = h_new
        return h_new

    def norm(rs, h_new):
        u2_ref[rs, :] = _modulated_norm(h_new, g2_ref[...], scale2_ref[...],
                                        shift2_ref[...]).astype(BF16)

    for rs in parts:
        combine(rs)
    for rs, h_new in [(rs, project(rs)) for rs in parts]:
        norm(rs, h_new)


def _merge(h, o_groups, lse_groups, o_b, gates, mod3, gate_idx, g2, shift2_idx, scale2_idx,
           wa, wb, wo, *, seq, tm=512):
    tokens, d_model = h.shape
    blocks_per_seq = seq // tm
    row = lambda i: (i, 0)
    resident = lambda w: pl.BlockSpec(w.shape, lambda i: (0, 0), pipeline_mode=pl.Buffered(1))
    mod_spec = lambda idx: pl.BlockSpec((None, 1, d_model),
                                        lambda i: (i // blocks_per_seq, 0, idx))
    return pl.pallas_call(
        _merge_kernel,
        grid=(tokens // tm,),
        in_specs=[
            pl.BlockSpec((tm, d_model), row),
            pl.BlockSpec((tm, GROUP_W), row),
            pl.BlockSpec((tm, GROUP_W), row),
            pl.BlockSpec((tm, GROUP_W), row),
            pl.BlockSpec((tm, HEAD_DIM), row),
            pl.BlockSpec((tm, HEAD_DIM), row),
            pl.BlockSpec((tm, HEAD_DIM), row),
            pl.BlockSpec((tm, GROUP_W), row),
            pl.BlockSpec((tm, d_model), lambda i: (i, 0)),
            pl.BlockSpec((tm, d_model), lambda i: (i, 1)),
            mod_spec(gate_idx),
            pl.BlockSpec((1, d_model), lambda i: (0, 0)),
            mod_spec(scale2_idx),
            mod_spec(shift2_idx),
            resident(wa),
            resident(wb),
            resident(wo),
        ],
        out_specs=[pl.BlockSpec((tm, d_model), row), pl.BlockSpec((tm, d_model), row)],
        out_shape=[jax.ShapeDtypeStruct((tokens, d_model), F32),
                   jax.ShapeDtypeStruct((tokens, d_model), BF16)],
        scratch_shapes=[pltpu.VMEM((tm, GROUP_W), BF16)],
        compiler_params=_params("parallel"),
        name="merge",
    )(h, *o_groups, *lse_groups, o_b, gates, gates, mod3, g2, mod3, mod3, wa, wb, wo)


def _ffn_kernel(u_ref, hres_ref, gate_ref, wg_ref, wu_ref, wd_ref, *refs, nf, tf, n_cast):
    cast_in, out_ref, cast_out, act_ref = refs[:n_cast], refs[n_cast], refs[n_cast + 1:-1], refs[-1]
    s = pl.program_id(1)

    @pl.when(s < nf)
    def _():
        u = u_ref[...]
        for c0 in range(0, tf, MXU_WIDTH):
            gate_act = _dot(u, wg_ref[:, c0:c0 + MXU_WIDTH])
            up = _dot(u, wu_ref[:, c0:c0 + MXU_WIDTH])
            cols = pl.ds(pl.multiple_of(s * tf + c0, MXU_WIDTH), MXU_WIDTH)
            act_ref[:, cols] = (gate_act * jax.nn.sigmoid(gate_act) * up).astype(BF16)
        _run_casts(cast_in, cast_out)

    @pl.when(s >= nf)
    def _():
        out_ref[...] = hres_ref[...] + gate_ref[...] * _dot(act_ref[...], wd_ref[...])
        _run_casts(cast_in, cast_out)


def _ffn(h, u2, mod3, gate_idx, w_gate_up, w_down, casts=(), *, seq, tm=1024, tf=512, tn=512):
    tokens, d_model = h.shape
    d_ff = w_down.shape[0]
    nf = d_ff // tf
    nd = d_model // tn
    steps = nf + nd
    blocks_per_seq = seq // tm
    up_step = lambda s: jnp.minimum(s, nf - 1)
    down_step = lambda s: jnp.clip(s - nf, 0, nd - 1)
    cast_in, cast_out, cast_shapes = _cast_specs(casts, (tokens // tm) * steps,
                                                 lambda i, s: i * steps + s)
    kern = functools.partial(_ffn_kernel, nf=nf, tf=tf, n_cast=len(casts))
    out, *converted = pl.pallas_call(
        kern,
        grid=(tokens // tm, steps),
        in_specs=[
            pl.BlockSpec((tm, d_model), lambda i, s: (i, 0)),
            pl.BlockSpec((tm, tn), lambda i, s: (i, down_step(s))),
            pl.BlockSpec((None, 1, tn),
                         lambda i, s: (i // blocks_per_seq, 0, gate_idx * nd + down_step(s))),
            pl.BlockSpec((d_model, tf), lambda i, s: (0, up_step(s))),
            pl.BlockSpec((d_model, tf), lambda i, s: (0, nf + up_step(s))),
            pl.BlockSpec((d_ff, tn), lambda i, s: (0, down_step(s))),
        ] + cast_in,
        out_specs=[pl.BlockSpec((tm, tn), lambda i, s: (i, down_step(s)))] + cast_out,
        out_shape=[jax.ShapeDtypeStruct((tokens, d_model), F32)] + cast_shapes,
        scratch_shapes=[pltpu.VMEM((tm, d_ff), BF16)],
        compiler_params=_params("arbitrary", "arbitrary"),
        name="ffn",
    )(u2, h, mod3, w_gate_up, w_gate_up, w_down, *[w_f32 for w_f32, _ in casts])
    return out, converted


def _rope_tables(seq):
    inv = jnp.power(ROPE_THETA, -jnp.arange(0, HEAD_DIM, 2, dtype=F32) / HEAD_DIM)
    ang = jnp.arange(seq, dtype=F32)[:, None] * inv[None, :]
    cos, sin = jnp.cos(ang), jnp.sin(ang)
    return jnp.concatenate([cos, cos], axis=-1), jnp.concatenate([-sin, sin], axis=-1)


def kernel(x, c, w_ada, b_ada, norm1_g, norm2_g, w_in, qn_g, kn_g, w_branch_a, w_branch_b,
           w_out, w_gate_up, w_down):
    bsz, seq, d_model = x.shape
    depth = w_ada.shape[0]
    tokens = bsz * seq
    cos_full, sin_full = _rope_tables(seq)
    mod = _modulation(c, w_ada, b_ada)
    w_in_l = w_in[0].astype(BF16)
    h = x.reshape(tokens, d_model)
    for l in range(depth):
        mod3 = mod[l].reshape(bsz, 1, 6 * d_model)
        u = _norm_modulate(h, norm1_g[l].reshape(1, d_model), mod3, 0, 1, seq)
        has_next = l + 1 < depth
        nxt = lambda *ws: [(w, l + 1) for w in ws] if has_next else []
        first = lambda *ws: [(w, 0) for w in ws] if l == 0 else []
        casts = (first(w_gate_up), first(w_down),
                 first(w_out, w_branch_a, w_branch_b) + nxt(w_in))
        qk, vb, gates, converted = _input_projections(u, w_in_l, cos_full, sin_full, qn_g[l],
                                                      kn_g[l], casts, seq=seq)
        if l == 0:
            (w_gate_up_l,), (w_down_l,) = converted[:2]
            w_out_l, w_branch_a_l, w_branch_b_l = converted[2][:3]
        outs = [_dilated_attention(qk, vb, bsz=bsz, seq=seq, group=g) for g in range(N_GROUPS)]
        o_b, next_gate_up = _stick_breaking(vb, nxt(w_gate_up), bsz=bsz, seq=seq)
        h, u2 = _merge(h, [o for o, _ in outs], [s for _, s in outs], o_b, gates, mod3, 2,
                       norm2_g[l].reshape(1, d_model), 3, 4,
                       w_branch_a_l, w_branch_b_l, w_out_l, seq=seq)
        h, next_rest = _ffn(h, u2, mod3, 5, w_gate_up_l, w_down_l,
                            nxt(w_down, w_out, w_branch_a, w_branch_b), seq=seq)
        if has_next:
            w_in_l, (w_gate_up_l,) = converted[2][-1], next_gate_up
            w_down_l, w_out_l, w_branch_a_l, w_branch_b_l = next_rest
    return h.reshape(bsz, seq, d_model)
```
